```python
import math
import jax, jax.numpy as jnp
from jax import lax
import numpy as np

D_MODEL = 1024
BATCH = 16
SEQ = 2048
DEPTH = 1
DEC_BATCH = 8
DEC_SEQ = 4096
PAST_LEN = 128

MIX_WIDTH = D_MODEL
ATT_WIDTH = MIX_WIDTH // 2
LRU_WIDTH = MIX_WIDTH - ATT_WIDTH
N_ATT_HEADS = 4
HEAD_DV = ATT_WIDTH // N_ATT_HEADS
HEAD_DK = HEAD_DV // 2
QK_WIDTH = N_ATT_HEADS * 2 * HEAD_DK
N_LRU_BLOCKS = 8
LRU_BLOCK = LRU_WIDTH // N_LRU_BLOCKS
CONV_W = 4
CONV_PAD_L = 2
RG_C = 8.0
D_FF = ((8 * D_MODEL // 3 + 255) // 256) * 256
IN_WIDTH = 2 * QK_WIDTH + ATT_WIDTH + 2 * LRU_WIDTH
Q_BLOCK = 128
NORM_EPS = 1e-6

kernel_name = "hymba_diffattn_rglru_encoder"


def rmsnorm(x, g):
    xf = x.astype(jnp.float32)
    y = xf * lax.rsqrt(jnp.mean(xf * xf, axis=-1, keepdims=True) + NORM_EPS)
    return (y * g.astype(jnp.float32)).astype(x.dtype)


def alibi_slopes(n):
    return jnp.asarray([2.0 ** (-8.0 * (h + 1) / n) for h in range(n)], dtype=jnp.float32)


def diff_attention(q, k, v, lam):
    B, S = q.shape[0], q.shape[1]
    nb = S // Q_BLOCK
    scale = 1.0 / math.sqrt(HEAD_DK)
    slopes = alibi_slopes(N_ATT_HEADS)
    qb = q.reshape(B, nb, Q_BLOCK, N_ATT_HEADS, 2, HEAD_DK).transpose(1, 0, 2, 3, 4, 5)
    starts = jnp.arange(nb, dtype=jnp.int32) * Q_BLOCK
    kpos = jnp.arange(S, dtype=jnp.int32)

    def block(args):
        qblk, start = args
        s = jnp.einsum('bqhmd,bkhmd->bhmqk', qblk, k,
                       preferred_element_type=jnp.float32) * scale
        qpos = start + jnp.arange(Q_BLOCK, dtype=jnp.int32)
        dist = jnp.abs(qpos[:, None] - kpos[None, :]).astype(jnp.float32)
        s = s - slopes[None, :, None, None, None] * dist[None, None, None]
        p = jax.nn.softmax(s, axis=-1)
        w = p[:, :, 0] - lam * p[:, :, 1]
        return jnp.einsum('bhqk,bkhd->bqhd', w.astype(v.dtype), v)

    o = lax.map(block, (qb, starts))
    return o.transpose(1, 0, 2, 3, 4).reshape(B, S, N_ATT_HEADS, HEAD_DV)


def centred_dwconv(x, w, b):
    S = x.shape[1]
    xp = jnp.pad(x, ((0, 0), (CONV_PAD_L, CONV_W - 1 - CONV_PAD_L), (0, 0)))
    out = xp[:, 0:S] * w[0]
    for j in range(1, CONV_W):
        out = out + xp[:, j:j + S] * w[j]
    return out + b


def _lin_combine(left, right):
    a1, b1 = left
    a2, b2 = right
    return a1 * a2, a2 * b1 + b2


def rg_lru(x, w_r, b_r, w_i, b_i, lam, reverse):
    B, S, W = x.shape
    xf = x.astype(jnp.float32)
    xb = xf.reshape(B, S, N_LRU_BLOCKS, LRU_BLOCK)
    r = jax.nn.sigmoid(jnp.einsum('bsnc,ncd->bsnd', xb, w_r.astype(jnp.float32)).reshape(B, S, W)
                       + b_r.astype(jnp.float32))
    i = jax.nn.sigmoid(jnp.einsum('bsnc,ncd->bsnd', xb, w_i.astype(jnp.float32)).reshape(B, S, W)
                       + b_i.astype(jnp.float32))
    log_a = -RG_C * r * jax.nn.softplus(-lam.astype(jnp.float32))
    a = jnp.exp(log_a)
    mult = jnp.sqrt(jnp.maximum(-jnp.expm1(2.0 * log_a), 1e-12))
    bterm = mult * (i * xf)
    _, h = lax.associative_scan(_lin_combine, (a, bterm), axis=1, reverse=reverse)
    return h


def encoder_layer(x, layer, norm_mix, w_in, conv_w, conv_b, w_rg, b_rg, w_ig, b_ig,
                  lru_lambda, lambda_q1, lambda_k1, lambda_q2, lambda_k2, subln_g,
                  w_out, norm_ffn, w_gate, w_up, w_down):
    B, S, _ = x.shape
    lam_init = 0.8 - 0.6 * math.exp(-0.3 * layer)
    h = rmsnorm(x, norm_mix)
    proj = h @ w_in
    q, k, v, xr, gate = jnp.split(
        proj, [QK_WIDTH, 2 * QK_WIDTH, 2 * QK_WIDTH + ATT_WIDTH,
               2 * QK_WIDTH + ATT_WIDTH + LRU_WIDTH], axis=-1)
    q = q.reshape(B, S, N_ATT_HEADS, 2, HEAD_DK)
    k = k.reshape(B, S, N_ATT_HEADS, 2, HEAD_DK)
    v = v.reshape(B, S, N_ATT_HEADS, HEAD_DV)
    f32 = jnp.float32
    lam = (jnp.exp(jnp.sum(lambda_q1.astype(f32) * lambda_k1.astype(f32)))
           - jnp.exp(jnp.sum(lambda_q2.astype(f32) * lambda_k2.astype(f32))) + lam_init)
    o = diff_attention(q, k, v, lam)
    o = rmsnorm(o, subln_g) * (1.0 - lam_init)
    o = o.reshape(B, S, ATT_WIDTH)
    xc = centred_dwconv(xr, conv_w, conv_b)
    hl = (rg_lru(xc, w_rg[0], b_rg[0], w_ig[0], b_ig[0], lru_lambda[0], False)
          + rg_lru(xc, w_rg[1], b_rg[1], w_ig[1], b_ig[1], lru_lambda[1], True))
    y_lru = hl.astype(x.dtype) * jax.nn.gelu(gate)
    x = x + jnp.concatenate([o.astype(x.dtype), y_lru], axis=-1) @ w_out
    h2 = rmsnorm(x, norm_ffn)
    x = x + (jax.nn.silu(h2 @ w_gate) * (h2 @ w_up)) @ w_down
    return x


def run_trunk(x, norm_mix, w_in, conv_w, conv_b, w_rg, b_rg, w_ig, b_ig, lru_lambda,
              lambda_q1, lambda_k1, lambda_q2, lambda_k2, subln_g, w_out, norm_ffn,
              w_gate, w_up, w_down, norm_final):
    for l in range(DEPTH):
        x = encoder_layer(x, l, norm_mix[l], w_in[l], conv_w[l], conv_b[l], w_rg[l], b_rg[l],
                          w_ig[l], b_ig[l], lru_lambda[l], lambda_q1[l], lambda_k1[l],
                          lambda_q2[l], lambda_k2[l], subln_g[l], w_out[l], norm_ffn[l],
                          w_gate[l], w_up[l], w_down[l])
    return rmsnorm(x, norm_final)


def setup_inputs(seed: int = 0) -> dict:
    key = jax.random.key(seed)
    ks = jax.random.split(key, 24)
    f32 = jnp.float32
    nrm = lambda k, shape, s: jax.random.normal(k, shape, f32) * s
    a8 = jax.random.uniform(ks[12], (DEPTH, 2, LRU_WIDTH), f32, 0.9, 0.999)
    a = a8 ** (1.0 / RG_C)
    lru_lambda = jnp.log(a) - jnp.log1p(-a)
    return {
        "x_prompt": nrm(ks[0], (BATCH, SEQ, D_MODEL), 1.0),
        "x_sample": nrm(ks[1], (DEC_BATCH, DEC_SEQ, D_MODEL), 1.0),
        "norm_mix": 1.0 + nrm(ks[2], (DEPTH, D_MODEL), 0.02),
        "w_in": nrm(ks[3], (DEPTH, D_MODEL, IN_WIDTH), D_MODEL ** -0.5),
        "conv_w": nrm(ks[4], (DEPTH, CONV_W, LRU_WIDTH), CONV_W ** -0.5),
        "conv_b": nrm(ks[5], (DEPTH, LRU_WIDTH), 0.02),
        "w_rg": nrm(ks[6], (DEPTH, 2, N_LRU_BLOCKS, LRU_BLOCK, LRU_BLOCK), LRU_BLOCK ** -0.5),
        "b_rg": nrm(ks[7], (DEPTH, 2, LRU_WIDTH), 0.02),
        "w_ig": nrm(ks[8], (DEPTH, 2, N_LRU_BLOCKS, LRU_BLOCK, LRU_BLOCK), LRU_BLOCK ** -0.5),
        "b_ig": nrm(ks[9], (DEPTH, 2, LRU_WIDTH), 0.02),
        "lru_lambda": lru_lambda,
        "lambda_q1": nrm(ks[10], (DEPTH, HEAD_DK), 0.1),
        "lambda_k1": nrm(ks[11], (DEPTH, HEAD_DK), 0.1),
        "lambda_q2": nrm(ks[13], (DEPTH, HEAD_DK), 0.1),
        "lambda_k2": nrm(ks[14], (DEPTH, HEAD_DK), 0.1),
        "subln_g": 1.0 + nrm(ks[15], (DEPTH, HEAD_DV), 0.02),
        "w_out": nrm(ks[16], (DEPTH, MIX_WIDTH, D_MODEL), MIX_WIDTH ** -0.5),
        "norm_ffn": 1.0 + nrm(ks[17], (DEPTH, D_MODEL), 0.02),
        "w_gate": nrm(ks[18], (DEPTH, D_MODEL, D_FF), D_MODEL ** -0.5),
        "w_up": nrm(ks[19], (DEPTH, D_MODEL, D_FF), D_MODEL ** -0.5),
        "w_down": nrm(ks[20], (DEPTH, D_FF, D_MODEL), D_FF ** -0.5),
        "norm_final": 1.0 + nrm(ks[21], (D_MODEL,), 0.02),
    }


def reference(x_prompt, x_sample, norm_mix, w_in, conv_w, conv_b, w_rg, b_rg, w_ig, b_ig,
              lru_lambda, lambda_q1, lambda_k1, lambda_q2, lambda_k2, subln_g, w_out,
              norm_ffn, w_gate, w_up, w_down, norm_final):
    y_prompt = run_trunk(x_prompt, norm_mix, w_in, conv_w, conv_b, w_rg, b_rg, w_ig, b_ig,
                         lru_lambda, lambda_q1, lambda_k1, lambda_q2, lambda_k2, subln_g,
                         w_out, norm_ffn, w_gate, w_up, w_down, norm_final)
    y_sample = run_trunk(x_sample, norm_mix, w_in, conv_w, conv_b, w_rg, b_rg, w_ig, b_ig,
                         lru_lambda, lambda_q1, lambda_k1, lambda_q2, lambda_k2, subln_g,
                         w_out, norm_ffn, w_gate, w_up, w_down, norm_final)
    return (y_prompt, y_sample)
```

```python
import functools
import math

import jax
import jax.numpy as jnp
from jax import lax
from jax.experimental import pallas as pl
from jax.experimental.pallas import tpu as pltpu

F32 = jnp.float32
BF16 = jnp.bfloat16

N_HEADS = 4
HEAD_DV = 128
HEAD_DK = 64
N_LRU_BLOCKS = 8
CONV_W = 4
RG_C = 8.0
NORM_EPS = 1e-6
LANES = 128
SUBLANES = 8
VMEM_LIMIT = 56 * 1024 * 1024


def _rmsnorm(x, g):
    ms = jnp.mean(x * x, axis=-1, keepdims=True)
    return (x * lax.rsqrt(ms + NORM_EPS)) * g


def _inproj_kernel(x_ref, g_ref, w_ref, q_ref, k_ref, v_ref, xr_ref, gate_ref, *, qk_w, att_w, lru_w):
    h = _rmsnorm(x_ref[...], g_ref[...]).astype(BF16)

    def proj(lo, width):
        return jnp.dot(h, w_ref[:, lo:lo + width], preferred_element_type=F32)

    q_ref[...] = (proj(0, qk_w) * (1.0 / math.sqrt(HEAD_DK))).astype(BF16)
    k_ref[...] = proj(qk_w, qk_w).astype(BF16)
    v_ref[...] = proj(2 * qk_w, att_w).astype(BF16)
    xr_ref[...] = proj(2 * qk_w + att_w, lru_w)
    gate_ref[...] = proj(2 * qk_w + att_w + lru_w, lru_w)


def _inproj(x2d, norm_g, w_in_bf16, *, tm):
    n_tok, d_model = x2d.shape
    qk_w = N_HEADS * 2 * HEAD_DK
    att_w = N_HEADS * HEAD_DV
    lru_w = (w_in_bf16.shape[1] - 2 * qk_w - att_w) // 2
    grid = (n_tok // tm,)
    row = lambda i: (i, 0)
    const = lambda i: (0, 0)
    return pl.pallas_call(
        functools.partial(_inproj_kernel, qk_w=qk_w, att_w=att_w, lru_w=lru_w),
        grid=grid,
        in_specs=[
            pl.BlockSpec((tm, d_model), row),
            pl.BlockSpec((1, d_model), const),
            pl.BlockSpec(w_in_bf16.shape, const, pipeline_mode=pl.Buffered(1)),
        ],
        out_specs=[
            pl.BlockSpec((tm, qk_w), row),
            pl.BlockSpec((tm, qk_w), row),
            pl.BlockSpec((tm, att_w), row),
            pl.BlockSpec((tm, lru_w), row),
            pl.BlockSpec((tm, lru_w), row),
        ],
        out_shape=[
            jax.ShapeDtypeStruct((n_tok, qk_w), BF16),
            jax.ShapeDtypeStruct((n_tok, qk_w), BF16),
            jax.ShapeDtypeStruct((n_tok, att_w), BF16),
            jax.ShapeDtypeStruct((n_tok, lru_w), F32),
            jax.ShapeDtypeStruct((n_tok, lru_w), F32),
        ],
        compiler_params=pltpu.CompilerParams(
            dimension_semantics=("arbitrary",), vmem_limit_bytes=VMEM_LIMIT),
        name="inproj",
    )(x2d, norm_g.reshape(1, d_model), w_in_bf16)


def _attn_kernel(slopes_ref, q_ref, k_ref, v_ref, lq1_ref, lk1_ref, lq2_ref, lk2_ref, g_ref,
                 o_ref, bias_ref, *, seq, tq, tk, lam_init):
    h = pl.program_id(0)
    b = pl.program_id(1)
    qi = pl.program_id(2)
    n_kc = seq // tk
    tab_w = 2 * seq - tq

    @pl.when((b == 0) & (qi == 0))
    def _():
        slope = slopes_ref[h]
        r = lax.broadcasted_iota(jnp.int32, (tq, tab_w), 0)
        u = lax.broadcasted_iota(jnp.int32, (tq, tab_w), 1)
        bias_ref[...] = -slope * jnp.abs(r - u + (seq - tq)).astype(F32)

    q = q_ref[0]
    lane = lax.broadcasted_iota(jnp.int32, q.shape, 1)
    zero = jnp.zeros_like(q)
    qm = (jnp.where(lane < HEAD_DK, q, zero), jnp.where(lane >= HEAD_DK, q, zero))
    col0 = seq - tq - qi * tq

    def chunk(c, carry):
        ks = pl.multiple_of(c * tk, tk)
        kc = k_ref[0, pl.ds(ks, tk), :]
        vc = v_ref[0, pl.ds(ks, tk), :]
        bias = bias_ref[:, pl.ds(pl.multiple_of(col0 + ks, LANES), tk)]
        out = []
        for mp in range(2):
            m_old, l_old, acc_old = carry[mp]
            s = lax.dot_general(qm[mp], kc, (((1,), (1,)), ((), ())),
                                preferred_element_type=F32) + bias
            m_new = jnp.maximum(m_old, jnp.max(s, axis=-1, keepdims=True))
            alpha = jnp.exp(m_old - m_new)
            p = jnp.exp(s - m_new)
            l_new = alpha * l_old + jnp.sum(p, axis=-1, keepdims=True)
            acc_new = alpha * acc_old + jnp.dot(p.astype(BF16), vc, preferred_element_type=F32)
            out.append((m_new, l_new, acc_new))
        return tuple(out)

    init = tuple((jnp.full((tq, 1), -jnp.inf, F32), jnp.zeros((tq, 1), F32),
                  jnp.zeros((tq, HEAD_DV), F32)) for _ in range(2))
    (m0, l0, acc0), (m1, l1, acc1) = lax.fori_loop(0, n_kc, chunk, init)

    lam = (jnp.exp(jnp.sum(lq1_ref[...] * lk1_ref[...], axis=-1, keepdims=True))
           - jnp.exp(jnp.sum(lq2_ref[...] * lk2_ref[...], axis=-1, keepdims=True)) + lam_init)
    o = acc0 / l0 - lam * (acc1 / l1)
    o_ref[0] = (_rmsnorm(o, g_ref[...]) * (1.0 - lam_init)).astype(BF16)


def _attention(q, k, v, lq1, lk1, lq2, lk2, subln_g, *, lam_init, tq, tk):
    batch, seq, _ = q.shape
    slopes = jnp.asarray([2.0 ** (-8.0 * (h + 1) / N_HEADS) for h in range(N_HEADS)], F32)
    grid = (N_HEADS, batch, seq // tq)
    vec = lambda a: a.reshape(1, -1).astype(F32)
    small = lambda n: pl.BlockSpec((1, n), lambda h, b, i: (0, 0))
    return pl.pallas_call(
        functools.partial(_attn_kernel, seq=seq, tq=tq, tk=tk, lam_init=lam_init),
        grid=grid,
        in_specs=[
            pl.BlockSpec(memory_space=pltpu.SMEM),
            pl.BlockSpec((1, tq, HEAD_DV), lambda h, b, i: (b, i, h)),
            pl.BlockSpec((1, seq, HEAD_DV), lambda h, b, i: (b, 0, h)),
            pl.BlockSpec((1, seq, HEAD_DV), lambda h, b, i: (b, 0, h)),
            small(HEAD_DK), small(HEAD_DK), small(HEAD_DK), small(HEAD_DK), small(HEAD_DV),
        ],
        out_specs=pl.BlockSpec((1, tq, HEAD_DV), lambda h, b, i: (b, i, h)),
        out_shape=jax.ShapeDtypeStruct((batch, seq, N_HEADS * HEAD_DV), BF16),
        scratch_shapes=[pltpu.VMEM((tq, 2 * seq - tq), F32)],
        compiler_params=pltpu.CompilerParams(
            dimension_semantics=("arbitrary", "arbitrary", "arbitrary"),
            vmem_limit_bytes=VMEM_LIMIT),
        name="diffattn",
    )(slopes, q, k, v, vec(lq1), vec(lk1), vec(lq2), vec(lk2), vec(subln_g))


def _sigmoid(x):
    return 1.0 / (1.0 + jnp.exp(-x))


def _softplus(x):
    return jnp.maximum(x, 0.0) + jnp.log1p(jnp.exp(-jnp.abs(x)))


def _gelu_tanh(x):
    return 0.5 * x * (1.0 + jnp.tanh(math.sqrt(2.0 / math.pi) * (x + 0.044715 * (x * x * x))))


def _lru_kernel(xr_ref, gate_ref, cw_ref, cb_ref, wg_ref, bg_ref, lam_ref, y_ref,
                xpad_ref, a_ref, b_ref, *, seq, tc):
    seg = seq // SUBLANES
    pad = SUBLANES

    xpad_ref[0:pad, :] = jnp.zeros((pad, LANES), F32)
    xpad_ref[pad + seq:pad + seq + pad, :] = jnp.zeros((pad, LANES), F32)
    xpad_ref[pad:pad + seq, :] = xr_ref[0]

    cw = cw_ref[...]
    cb = cb_ref[...]
    sp = _softplus(-lam_ref[0])

    def gates(c, _):
        t0 = pl.multiple_of(c * tc, tc)
        xc = cb
        for j in range(CONV_W):
            xc = xc + xpad_ref[pl.ds(t0 + pad - 2 + j, tc), :] * cw[j:j + 1, :]
        pre = jnp.dot(xc.astype(BF16), wg_ref[0], preferred_element_type=F32) + bg_ref[0]
        for d in range(2):
            r = _sigmoid(pre[:, (2 * d) * LANES:(2 * d + 1) * LANES])
            i = _sigmoid(pre[:, (2 * d + 1) * LANES:(2 * d + 2) * LANES])
            log_a = (-RG_C) * r * sp[d:d + 1, :]
            a = jnp.exp(log_a)
            mult = jnp.sqrt(jnp.maximum(1.0 - a * a, 1e-12))
            a_ref[d, pl.ds(t0, tc), :] = a
            b_ref[d, pl.ds(t0, tc), :] = mult * (i * xc)
        return 0

    lax.fori_loop(0, seq // tc, gates, 0)

    def rows(t):
        return pl.ds(t, SUBLANES, stride=seg)

    def local_scan(d, reverse):
        def body(n, carry):
            hst, prod = carry
            t = (seg - 1 - n) if reverse else n
            a = a_ref[d, rows(t), :]
            hst = a * hst + b_ref[d, rows(t), :]
            prod = a * prod
            b_ref[d, rows(t), :] = hst
            a_ref[d, rows(t), :] = prod
            return hst, prod
        return lax.fori_loop(0, seg, body, (jnp.zeros((SUBLANES, LANES), F32),
                                            jnp.ones((SUBLANES, LANES), F32)))

    def chain(h_end, p_end, reverse):
        row = lax.broadcasted_iota(jnp.int32, (SUBLANES, LANES), 0)
        edge = (SUBLANES - 1) if reverse else 0
        shift = (SUBLANES - 1) if reverse else 1
        carry = jnp.zeros((SUBLANES, LANES), F32)
        for _ in range(SUBLANES - 1):
            nxt = pltpu.roll(h_end + p_end * carry, shift, 0)
            carry = jnp.where(row == edge, 0.0, nxt)
        return carry

    def fixup(d, carry):
        def body(t, _):
            b_ref[d, rows(t), :] = b_ref[d, rows(t), :] + a_ref[d, rows(t), :] * carry
            return 0
        lax.fori_loop(0, seg, body, 0)

    for d, reverse in ((0, False), (1, True)):
        h_end, p_end = local_scan(d, reverse)
        fixup(d, chain(h_end, p_end, reverse))

    def finish(c, _):
        t0 = pl.multiple_of(c * tc, tc)
        hsum = b_ref[0, pl.ds(t0, tc), :] + b_ref[1, pl.ds(t0, tc), :]
        y_ref[0, pl.ds(t0, tc), :] = (hsum * _gelu_tanh(gate_ref[0, pl.ds(t0, tc), :])).astype(BF16)
        return 0

    lax.fori_loop(0, seq // tc, finish, 0)


def _lru_gate_weights(w_rg, b_rg, w_ig, b_ig):
    blk = w_rg.shape[-1]
    per = LANES // blk
    n_groups = N_LRU_BLOCKS // per

    def dense(w):
        w = w.reshape(n_groups, per, blk, blk)
        eye = jnp.eye(per, dtype=w.dtype)
        return jnp.einsum('gpcd,pq->gpcqd', w, eye).reshape(n_groups, LANES, LANES)

    wg = jnp.concatenate([dense(w_rg[0]), dense(w_ig[0]), dense(w_rg[1]), dense(w_ig[1])], axis=-1)
    bias = lambda v: v.reshape(n_groups, 1, LANES)
    bg = jnp.concatenate([bias(b_rg[0]), bias(b_ig[0]), bias(b_rg[1]), bias(b_ig[1])], axis=-1)
    return wg.astype(BF16), bg.astype(F32)


def _rglru(xr, gate, conv_w, conv_b, wg, bg, lru_lambda, *, tc):
    batch, seq, lru_w = xr.shape
    n_groups = lru_w // LANES
    lam = lru_lambda.reshape(2, n_groups, LANES).transpose(1, 0, 2)
    grid = (batch, n_groups)
    tok = pl.BlockSpec((1, seq, LANES), lambda b, j: (b, 0, j))
    return pl.pallas_call(
        functools.partial(_lru_kernel, seq=seq, tc=tc),
        grid=grid,
        in_specs=[
            tok, tok,
            pl.BlockSpec((CONV_W, LANES), lambda b, j: (0, j)),
            pl.BlockSpec((1, LANES), lambda b, j: (0, j)),
            pl.BlockSpec((1, LANES, 4 * LANES), lambda b, j: (j, 0, 0)),
            pl.BlockSpec((1, 1, 4 * LANES), lambda b, j: (j, 0, 0)),
            pl.BlockSpec((1, 2, LANES), lambda b, j: (j, 0, 0)),
        ],
        out_specs=tok,
        out_shape=jax.ShapeDtypeStruct((batch, seq, lru_w), BF16),
        scratch_shapes=[
            pltpu.VMEM((seq + 2 * SUBLANES, LANES), F32),
            pltpu.VMEM((2, seq, LANES), F32),
            pltpu.VMEM((2, seq, LANES), F32),
        ],
        compiler_params=pltpu.CompilerParams(
            dimension_semantics=("arbitrary", "arbitrary"), vmem_limit_bytes=VMEM_LIMIT),
        name="rglru",
    )(xr, gate, conv_w, conv_b.reshape(1, lru_w), wg, bg, lam)


def _ffn_kernel(x_ref, o_ref, y_ref, wo_ref, g2_ref, wg_ref, wu_ref, wd_ref, g3_ref, out_ref, *, att_w):
    x1 = (x_ref[...]
          + jnp.dot(o_ref[...], wo_ref[0:att_w, :], preferred_element_type=F32)
          + jnp.dot(y_ref[...], wo_ref[att_w:, :], preferred_element_type=F32))
    h2 = _rmsnorm(x1, g2_ref[...]).astype(BF16)
    gt = jnp.dot(h2, wg_ref[...], preferred_element_type=F32)
    up = jnp.dot(h2, wu_ref[...], preferred_element_type=F32)
    act = ((gt * _sigmoid(gt)) * up).astype(BF16)
    x2 = x1 + jnp.dot(act, wd_ref[...], preferred_element_type=F32)
    out_ref[...] = _rmsnorm(x2, g3_ref[...])


def _out_ffn(x2d, o2d, y2d, w_out, norm_ffn, w_gate, w_up, w_down, norm_final, *, tm):
    n_tok, d_model = x2d.shape
    att_w = o2d.shape[1]
    lru_w = y2d.shape[1]
    row = lambda i: (i, 0)
    const = lambda i: (0, 0)
    resident = lambda a: pl.BlockSpec(a.shape, const, pipeline_mode=pl.Buffered(1))
    g2 = norm_ffn.reshape(1, d_model)
    g3 = norm_final.reshape(1, d_model)
    return pl.pallas_call(
        functools.partial(_ffn_kernel, att_w=att_w),
        grid=(n_tok // tm,),
        in_specs=[
            pl.BlockSpec((tm, d_model), row),
            pl.BlockSpec((tm, att_w), row),
            pl.BlockSpec((tm, lru_w), row),
            resident(w_out), resident(g2), resident(w_gate), resident(w_up), resident(w_down),
            resident(g3),
        ],
        out_specs=pl.BlockSpec((tm, d_model), row),
        out_shape=jax.ShapeDtypeStruct((n_tok, d_model), F32),
        compiler_params=pltpu.CompilerParams(
            dimension_semantics=("arbitrary",), vmem_limit_bytes=VMEM_LIMIT),
        name="out_ffn",
    )(x2d, o2d, y2d, w_out, g2, w_gate, w_up, w_down, g3)


def _trunk(x, p, *, tm_in, tq, tk, tc, tm_ffn):
    batch, seq, d_model = x.shape
    x2d = x.reshape(batch * seq, d_model)
    depth = p["w_in"].shape[0]
    for l in range(depth):
        last = l == depth - 1
        lam_init = 0.8 - 0.6 * math.exp(-0.3 * l)
        q, k, v, xr, gate = _inproj(x2d, p["norm_mix"][l], p["w_in"][l].astype(BF16), tm=tm_in)
        att_w = v.shape[1]
        lru_w = xr.shape[1]
        shp = lambda a: a.reshape(batch, seq, a.shape[1])
        o = _attention(shp(q), shp(k), shp(v), p["lambda_q1"][l], p["lambda_k1"][l],
                       p["lambda_q2"][l], p["lambda_k2"][l], p["subln_g"][l],
                       lam_init=lam_init, tq=tq, tk=tk)
        wg, bg = _lru_gate_weights(p["w_rg"][l], p["b_rg"][l], p["w_ig"][l], p["b_ig"][l])
        y = _rglru(shp(xr), shp(gate), p["conv_w"][l], p["conv_b"][l], wg, bg, p["lru_lambda"][l], tc=tc)
        assert last, "only the last layer's fused final norm is implemented"
        x2d = _out_ffn(x2d, o.reshape(batch * seq, att_w), y.reshape(batch * seq, lru_w),
                       p["w_out"][l].astype(BF16), p["norm_ffn"][l], p["w_gate"][l].astype(BF16),
                       p["w_up"][l].astype(BF16), p["w_down"][l].astype(BF16), p["norm_final"],
                       tm=tm_ffn)
    return x2d.reshape(batch, seq, d_model)


def _tiles(seq):
    return dict(tm_in=512, tq=min(256, seq), tk=min(512, seq), tc=min(256, seq // SUBLANES), tm_ffn=512)


def kernel(x_prompt, x_sample, norm_mix, w_in, conv_w, conv_b, w_rg, b_rg, w_ig, b_ig, lru_lambda, lambda_q1, lambda_k1, lambda_q2, lambda_k2, subln_g, w_out, norm_ffn, w_gate, w_up, w_down, norm_final):
    p = dict(norm_mix=norm_mix, w_in=w_in, conv_w=conv_w, conv_b=conv_b, w_rg=w_rg, b_rg=b_rg,
             w_ig=w_ig, b_ig=b_ig, lru_lambda=lru_lambda, lambda_q1=lambda_q1, lambda_k1=lambda_k1,
             lambda_q2=lambda_q2, lambda_k2=lambda_k2, subln_g=subln_g, w_out=w_out,
             norm_ffn=norm_ffn, w_gate=w_gate, w_up=w_up, w_down=w_down, norm_final=norm_final)
    y_prompt = _trunk(x_prompt, p, **_tiles(x_prompt.shape[1]))
    y_sample = _trunk(x_sample, p, **_tiles(x_sample.shape[1]))
    return (y_prompt, y_sample)
```

```python
import functools
import math

import jax
import jax.numpy as jnp
from jax import lax
from jax.experimental import pallas as pl
from jax.experimental.pallas import tpu as pltpu

F32 = jnp.float32
BF16 = jnp.bfloat16

N_HEADS = 4
HEAD_DV = 128
HEAD_DK = 64
N_LRU_BLOCKS = 8
CONV_W = 4
RG_C = 8.0
NORM_EPS = 1e-6
LOG2E = math.log2(math.e)
LANES = 128
SUBLANES = 8
VMEM_LIMIT = 56 * 1024 * 1024


def _rmsnorm(x, g):
    ms = jnp.mean(x * x, axis=-1, keepdims=True)
    return (x * lax.rsqrt(ms + NORM_EPS)) * g


def _inproj_kernel(x_ref, g_ref, w_ref, q_ref, k_ref, v_ref, xr_ref, gate_ref, *, qk_w, att_w, lru_w):
    h = _rmsnorm(x_ref[...], g_ref[...]).astype(BF16)

    def proj(lo, width):
        return jnp.dot(h, w_ref[:, lo:lo + width], preferred_element_type=F32)

    q_ref[...] = (proj(0, qk_w) * (LOG2E / math.sqrt(HEAD_DK))).astype(BF16)
    k_ref[...] = proj(qk_w, qk_w).astype(BF16)
    v_ref[...] = proj(2 * qk_w, att_w).astype(BF16)
    xr_ref[...] = proj(2 * qk_w + att_w, lru_w)
    gate_ref[...] = proj(2 * qk_w + att_w + lru_w, lru_w)


def _inproj(x2d, norm_g, w_in_bf16, *, tm):
    n_tok, d_model = x2d.shape
    qk_w = N_HEADS * 2 * HEAD_DK
    att_w = N_HEADS * HEAD_DV
    lru_w = (w_in_bf16.shape[1] - 2 * qk_w - att_w) // 2
    grid = (n_tok // tm,)
    row = lambda i: (i, 0)
    const = lambda i: (0, 0)
    return pl.pallas_call(
        functools.partial(_inproj_kernel, qk_w=qk_w, att_w=att_w, lru_w=lru_w),
        grid=grid,
        in_specs=[
            pl.BlockSpec((tm, d_model), row),
            pl.BlockSpec((1, d_model), const),
            pl.BlockSpec(w_in_bf16.shape, const, pipeline_mode=pl.Buffered(1)),
        ],
        out_specs=[
            pl.BlockSpec((tm, qk_w), row),
            pl.BlockSpec((tm, qk_w), row),
            pl.BlockSpec((tm, att_w), row),
            pl.BlockSpec((tm, lru_w), row),
            pl.BlockSpec((tm, lru_w), row),
        ],
        out_shape=[
            jax.ShapeDtypeStruct((n_tok, qk_w), BF16),
            jax.ShapeDtypeStruct((n_tok, qk_w), BF16),
            jax.ShapeDtypeStruct((n_tok, att_w), BF16),
            jax.ShapeDtypeStruct((n_tok, lru_w), F32),
            jax.ShapeDtypeStruct((n_tok, lru_w), F32),
        ],
        compiler_params=pltpu.CompilerParams(
            dimension_semantics=("arbitrary",), vmem_limit_bytes=VMEM_LIMIT),
        name="inproj",
    )(x2d, norm_g.reshape(1, d_model), w_in_bf16)


def _attn_kernel(slopes_ref, q_ref, k_ref, v_ref, lq1_ref, lk1_ref, lq2_ref, lk2_ref, g_ref,
                 o_ref, bias_ref, q2_ref, s_ref, p_ref, alpha_ref, m_ref, l_ref, acc_ref,
                 *, seq, tq, tk, lam_init):
    h = pl.program_id(0)
    b = pl.program_id(1)
    qi = pl.program_id(2)
    n_kc = seq // tk
    tab_w = 2 * seq - tq

    @pl.when((b == 0) & (qi == 0))
    def _():
        slope = slopes_ref[h]
        r = lax.broadcasted_iota(jnp.int32, (tq, tab_w), 0)
        u = lax.broadcasted_iota(jnp.int32, (tq, tab_w), 1)
        bias_ref[...] = (-LOG2E * slope) * jnp.abs(r - u + (seq - tq)).astype(F32)

    q = q_ref[0]
    lane = lax.broadcasted_iota(jnp.int32, q.shape, 1)
    zero = jnp.zeros_like(q)
    q2_ref[0:tq, :] = jnp.where(lane < HEAD_DK, q, zero)
    q2_ref[tq:2 * tq, :] = jnp.where(lane >= HEAD_DK, q, zero)
    col0 = seq - tq - qi * tq

    m_ref[...] = jnp.full(m_ref.shape, -jnp.inf, F32)
    l_ref[...] = jnp.zeros(l_ref.shape, F32)
    acc_ref[...] = jnp.zeros(acc_ref.shape, F32)

    def scores(c, slot):
        ks = pl.multiple_of(c * tk, tk)
        s = lax.dot_general(q2_ref[...], k_ref[0, pl.ds(ks, tk), :], (((1,), (1,)), ((), ())),
                            preferred_element_type=F32)
        bias = bias_ref[:, pl.ds(pl.multiple_of(col0 + ks, LANES), tk)]
        s_ref[slot] = s.reshape(2, tq, tk) + bias[None]

    def softmax(slot):
        tiles = [s_ref[slot, :, :, j * LANES:(j + 1) * LANES] for j in range(tk // LANES)]
        m_old = m_ref[...]
        m_new = jnp.maximum(m_old, jnp.max(functools.reduce(jnp.maximum, tiles), axis=-1, keepdims=True))
        alpha = jnp.exp2(m_old - m_new)
        l_new = alpha * l_ref[...]
        for j, s in enumerate(tiles):
            p = jnp.exp2(s - m_new)
            l_new = l_new + p
            p_ref[slot, :, :, j * LANES:(j + 1) * LANES] = p.astype(BF16)
        l_ref[...] = l_new
        m_ref[...] = m_new
        alpha_ref[slot] = alpha

    def values(c, slot):
        ks = pl.multiple_of(c * tk, tk)
        pv = jnp.dot(p_ref[slot].reshape(2 * tq, tk), v_ref[0, pl.ds(ks, tk), :],
                     preferred_element_type=F32)
        acc_ref[...] = alpha_ref[slot] * acc_ref[...] + pv.reshape(2, tq, HEAD_DV)

    def step(c, slot):
        values(c - 1, 1 - slot)
        softmax(slot)
        scores(c + 1, 1 - slot)

    scores(0, 0)
    if n_kc > 1:
        scores(1, 1)
    softmax(0)

    if n_kc > 2:
        assert n_kc % 2 == 0

        def two_steps(i, _):
            step(2 * i + 1, 1)
            step(2 * i + 2, 0)
            return 0

        lax.fori_loop(0, (n_kc - 2) // 2, two_steps, 0)
    if n_kc > 1:
        values(n_kc - 2, n_kc % 2)
        softmax((n_kc - 1) % 2)
    values(n_kc - 1, (n_kc - 1) % 2)

    lam = (jnp.exp(jnp.sum(lq1_ref[...] * lk1_ref[...], axis=-1, keepdims=True))
           - jnp.exp(jnp.sum(lq2_ref[...] * lk2_ref[...], axis=-1, keepdims=True)) + lam_init)
    l = jnp.sum(l_ref[...], axis=-1, keepdims=True)
    o = acc_ref[0] / l[0] - lam * (acc_ref[1] / l[1])
    o_ref[0] = (_rmsnorm(o, g_ref[...]) * (1.0 - lam_init)).astype(BF16)


def _attention(q, k, v, lq1, lk1, lq2, lk2, subln_g, *, lam_init, tq, tk):
    batch, seq, _ = q.shape
    slopes = jnp.asarray([2.0 ** (-8.0 * (h + 1) / N_HEADS) for h in range(N_HEADS)], F32)
    grid = (N_HEADS, batch, seq // tq)
    vec = lambda a: a.reshape(1, -1).astype(F32)
    small = lambda n: pl.BlockSpec((1, n), lambda h, b, i: (0, 0))
    return pl.pallas_call(
        functools.partial(_attn_kernel, seq=seq, tq=tq, tk=tk, lam_init=lam_init),
        grid=grid,
        in_specs=[
            pl.BlockSpec(memory_space=pltpu.SMEM),
            pl.BlockSpec((1, tq, HEAD_DV), lambda h, b, i: (b, i, h)),
            pl.BlockSpec((1, seq, HEAD_DV), lambda h, b, i: (b, 0, h)),
            pl.BlockSpec((1, seq, HEAD_DV), lambda h, b, i: (b, 0, h)),
            small(HEAD_DK), small(HEAD_DK), small(HEAD_DK), small(HEAD_DK), small(HEAD_DV),
        ],
        out_specs=pl.BlockSpec((1, tq, HEAD_DV), lambda h, b, i: (b, i, h)),
        out_shape=jax.ShapeDtypeStruct((batch, seq, N_HEADS * HEAD_DV), BF16),
        scratch_shapes=[
            pltpu.VMEM((tq, 2 * seq - tq), F32),
            pltpu.VMEM((2 * tq, HEAD_DV), BF16),
            pltpu.VMEM((2, 2, tq, tk), F32),
            pltpu.VMEM((2, 2, tq, tk), BF16),
            pltpu.VMEM((2, 2, tq, LANES), F32),
            pltpu.VMEM((2, tq, LANES), F32),
            pltpu.VMEM((2, tq, LANES), F32),
            pltpu.VMEM((2, tq, HEAD_DV), F32),
        ],
        compiler_params=pltpu.CompilerParams(
            dimension_semantics=("arbitrary", "arbitrary", "arbitrary"),
            vmem_limit_bytes=VMEM_LIMIT),
        name="diffattn",
    )(slopes, q, k, v, vec(lq1), vec(lk1), vec(lq2), vec(lk2), vec(subln_g))


def _sigmoid(x):
    return 1.0 / (1.0 + jnp.exp(-x))


def _softplus(x):
    return jnp.maximum(x, 0.0) + jnp.log1p(jnp.exp(-jnp.abs(x)))


def _lru_kernel(xr_ref, gate_ref, cw_ref, cb_ref, wg_ref, bg_ref, lam_ref, y_ref,
                xpad_ref, a_ref, b_ref, *, seq, tc, n_grp):
    seg = seq // (SUBLANES * n_grp)
    stride = seq // SUBLANES
    pad = SUBLANES

    xpad_ref[0:pad, :] = jnp.zeros((pad, LANES), F32)
    xpad_ref[pad + seq:pad + seq + pad, :] = jnp.zeros((pad, LANES), F32)
    xpad_ref[pad:pad + seq, :] = xr_ref[0]

    cw = cw_ref[...]
    cb = cb_ref[...]
    kk = _softplus(-lam_ref[0]) * (-0.5 * RG_C * LOG2E)

    def gates(c, _):
        t0 = pl.multiple_of(c * tc, tc)
        xc = cb
        for j in range(CONV_W):
            xc = xc + xpad_ref[pl.ds(t0 + pad - 2 + j, tc), :] * cw[j:j + 1, :]
        th = jnp.tanh(jnp.dot(xc.astype(BF16), wg_ref[0], preferred_element_type=F32) + bg_ref[0])
        xh = 0.5 * xc
        for d in range(2):
            t_r = th[:, (2 * d) * LANES:(2 * d + 1) * LANES]
            t_i = th[:, (2 * d + 1) * LANES:(2 * d + 2) * LANES]
            kd = kk[d:d + 1, :]
            a = jnp.exp2(kd * t_r + kd)
            m2 = jnp.maximum(1.0 - a * a, 1e-12)
            a_ref[d, pl.ds(t0, tc), :] = a
            b_ref[d, pl.ds(t0, tc), :] = (m2 * lax.rsqrt(m2)) * (t_i * xh + xh)
        return 0

    lax.fori_loop(0, seq // tc, gates, 0)

    def tile(g, t):
        return pl.ds(g * seg + t, SUBLANES, stride=stride)

    zeros = jnp.zeros((SUBLANES, LANES), F32)
    ones = jnp.ones((SUBLANES, LANES), F32)

    def pass1(n, carry):
        out = []
        for d in range(2):
            t = n if d == 0 else seg - 1 - n
            hs, ps = carry[d]
            nh, npr = [], []
            for g in range(n_grp):
                a = a_ref[d, tile(g, t), :]
                nh.append(a * hs[g] + b_ref[d, tile(g, t), :])
                npr.append(a * ps[g])
            out.append((tuple(nh), tuple(npr)))
        return tuple(out)

    ends = lax.fori_loop(0, seg, pass1,
                         tuple(((zeros,) * n_grp, (ones,) * n_grp) for _ in range(2)), unroll=2)

    def chain_rows(e, q, reverse):
        row = lax.broadcasted_iota(jnp.int32, (SUBLANES, LANES), 0)
        edge = (SUBLANES - 1) if reverse else 0
        shift = (SUBLANES - 1) if reverse else 1
        carry = zeros
        for _ in range(SUBLANES - 1):
            carry = jnp.where(row == edge, 0.0, pltpu.roll(e + q * carry, shift, 0))
        return carry

    def carries(h_end, p_end, reverse):
        order = range(n_grp - 1, -1, -1) if reverse else range(n_grp)
        local, prod = {}, {}
        c, q = zeros, ones
        for g in order:
            local[g], prod[g] = c, q
            c = h_end[g] + p_end[g] * c
            q = p_end[g] * q
        row_in = chain_rows(c, q, reverse)
        return tuple(local[g] + prod[g] * row_in for g in range(n_grp))

    def pass2(n, carry):
        out = []
        for d in range(2):
            t = n if d == 0 else seg - 1 - n
            nh = []
            for g in range(n_grp):
                hcur = a_ref[d, tile(g, t), :] * carry[d][g] + b_ref[d, tile(g, t), :]
                b_ref[d, tile(g, t), :] = hcur
                nh.append(hcur)
            out.append(tuple(nh))
        return tuple(out)

    lax.fori_loop(0, seg, pass2,
                  tuple(carries(ends[d][0], ends[d][1], d == 1) for d in range(2)), unroll=2)

    c1 = math.sqrt(2.0 / math.pi)

    def finish(c, _):
        t0 = pl.multiple_of(c * tc, tc)
        hsum = b_ref[0, pl.ds(t0, tc), :] + b_ref[1, pl.ds(t0, tc), :]
        x = gate_ref[0, pl.ds(t0, tc), :]
        th = jnp.tanh(x * ((x * x) * (c1 * 0.044715) + c1))
        xh = 0.5 * x
        y_ref[0, pl.ds(t0, tc), :] = (hsum * (xh * th + xh)).astype(BF16)
        return 0

    lax.fori_loop(0, seq // tc, finish, 0)


def _lru_gate_weights(w_rg, b_rg, w_ig, b_ig):
    blk = w_rg.shape[-1]
    per = LANES // blk
    n_groups = N_LRU_BLOCKS // per

    def dense(w):
        w = w.reshape(n_groups, per, blk, blk)
        eye = jnp.eye(per, dtype=w.dtype)
        return jnp.einsum('gpcd,pq->gpcqd', w, eye).reshape(n_groups, LANES, LANES)

    wg = jnp.concatenate([dense(w_rg[0]), dense(w_ig[0]), dense(w_rg[1]), dense(w_ig[1])], axis=-1)
    bias = lambda v: v.reshape(n_groups, 1, LANES)
    bg = jnp.concatenate([bias(b_rg[0]), bias(b_ig[0]), bias(b_rg[1]), bias(b_ig[1])], axis=-1)
    return (0.5 * wg).astype(BF16), (0.5 * bg).astype(F32)


def _rglru(xr, gate, conv_w, conv_b, wg, bg, lru_lambda, *, tc, n_grp):
    batch, seq, lru_w = xr.shape
    n_groups = lru_w // LANES
    lam = lru_lambda.reshape(2, n_groups, LANES).transpose(1, 0, 2)
    grid = (batch, n_groups)
    tok = pl.BlockSpec((1, seq, LANES), lambda b, j: (b, 0, j))
    return pl.pallas_call(
        functools.partial(_lru_kernel, seq=seq, tc=tc, n_grp=n_grp),
        grid=grid,
        in_specs=[
            tok, tok,
            pl.BlockSpec((CONV_W, LANES), lambda b, j: (0, j)),
            pl.BlockSpec((1, LANES), lambda b, j: (0, j)),
            pl.BlockSpec((1, LANES, 4 * LANES), lambda b, j: (j, 0, 0)),
            pl.BlockSpec((1, 1, 4 * LANES), lambda b, j: (j, 0, 0)),
            pl.BlockSpec((1, 2, LANES), lambda b, j: (j, 0, 0)),
        ],
        out_specs=tok,
        out_shape=jax.ShapeDtypeStruct((batch, seq, lru_w), BF16),
        scratch_shapes=[
            pltpu.VMEM((seq + 2 * SUBLANES, LANES), F32),
            pltpu.VMEM((2, seq, LANES), F32),
            pltpu.VMEM((2, seq, LANES), F32),
        ],
        compiler_params=pltpu.CompilerParams(
            dimension_semantics=("arbitrary", "arbitrary"), vmem_limit_bytes=VMEM_LIMIT),
        name="rglru",
    )(xr, gate, conv_w, conv_b.reshape(1, lru_w), wg, bg, lam)


def _ffn_kernel(x_ref, o_ref, y_ref, wo_ref, g2_ref, wg_ref, wu_ref, wd_ref, g3_ref, out_ref, *, att_w):
    x1 = (x_ref[...]
          + jnp.dot(o_ref[...], wo_ref[0:att_w, :], preferred_element_type=F32)
          + jnp.dot(y_ref[...], wo_ref[att_w:, :], preferred_element_type=F32))
    h2 = _rmsnorm(x1, g2_ref[...]).astype(BF16)
    gt = jnp.dot(h2, wg_ref[...], preferred_element_type=F32)
    up = jnp.dot(h2, wu_ref[...], preferred_element_type=F32)
    act = ((gt * _sigmoid(gt)) * up).astype(BF16)
    x2 = x1 + jnp.dot(act, wd_ref[...], preferred_element_type=F32)
    out_ref[...] = _rmsnorm(x2, g3_ref[...])


def _out_ffn(x2d, o2d, y2d, w_out, norm_ffn, w_gate, w_up, w_down, norm_final, *, tm):
    n_tok, d_model = x2d.shape
    att_w = o2d.shape[1]
    lru_w = y2d.shape[1]
    row = lambda i: (i, 0)
    const = lambda i: (0, 0)
    resident = lambda a: pl.BlockSpec(a.shape, const, pipeline_mode=pl.Buffered(1))
    g2 = norm_ffn.reshape(1, d_model)
    g3 = norm_final.reshape(1, d_model)
    return pl.pallas_call(
        functools.partial(_ffn_kernel, att_w=att_w),
        grid=(n_tok // tm,),
        in_specs=[
            pl.BlockSpec((tm, d_model), row),
            pl.BlockSpec((tm, att_w), row),
            pl.BlockSpec((tm, lru_w), row),
            resident(w_out), resident(g2), resident(w_gate), resident(w_up), resident(w_down),
            resident(g3),
        ],
        out_specs=pl.BlockSpec((tm, d_model), row),
        out_shape=jax.ShapeDtypeStruct((n_tok, d_model), F32),
        compiler_params=pltpu.CompilerParams(
            dimension_semantics=("arbitrary",), vmem_limit_bytes=VMEM_LIMIT),
        name="out_ffn",
    )(x2d, o2d, y2d, w_out, g2, w_gate, w_up, w_down, g3)


def _trunk(x, p, *, tm_in, tq, tk, tc, n_grp, tm_ffn):
    batch, seq, d_model = x.shape
    x2d = x.reshape(batch * seq, d_model)
    depth = p["w_in"].shape[0]
    for l in range(depth):
        last = l == depth - 1
        lam_init = 0.8 - 0.6 * math.exp(-0.3 * l)
        q, k, v, xr, gate = _inproj(x2d, p["norm_mix"][l], p["w_in"][l].astype(BF16), tm=tm_in)
        att_w = v.shape[1]
        lru_w = xr.shape[1]
        shp = lambda a: a.reshape(batch, seq, a.shape[1])
        o = _attention(shp(q), shp(k), shp(v), p["lambda_q1"][l], p["lambda_k1"][l],
                       p["lambda_q2"][l], p["lambda_k2"][l], p["subln_g"][l],
                       lam_init=lam_init, tq=tq, tk=tk)
        wg, bg = _lru_gate_weights(p["w_rg"][l], p["b_rg"][l], p["w_ig"][l], p["b_ig"][l])
        y = _rglru(shp(xr), shp(gate), p["conv_w"][l], p["conv_b"][l], wg, bg, p["lru_lambda"][l],
                   tc=tc, n_grp=n_grp)
        assert last, "only the last layer's fused final norm is implemented"
        x2d = _out_ffn(x2d, o.reshape(batch * seq, att_w), y.reshape(batch * seq, lru_w),
                       p["w_out"][l].astype(BF16), p["norm_ffn"][l], p["w_gate"][l].astype(BF16),
                       p["w_up"][l].astype(BF16), p["w_down"][l].astype(BF16), p["norm_final"],
                       tm=tm_ffn)
    return x2d.reshape(batch, seq, d_model)


def _tiles(seq):
    return dict(tm_in=512, tq=min(256, seq), tk=min(512, seq), tc=min(256, seq // SUBLANES), n_grp=4,
                tm_ffn=512)


def kernel(x_prompt, x_sample, norm_mix, w_in, conv_w, conv_b, w_rg, b_rg, w_ig, b_ig, lru_lambda, lambda_q1, lambda_k1, lambda_q2, lambda_k2, subln_g, w_out, norm_ffn, w_gate, w_up, w_down, norm_final):
    p = dict(norm_mix=norm_mix, w_in=w_in, conv_w=conv_w, conv_b=conv_b, w_rg=w_rg, b_rg=b_rg,
             w_ig=w_ig, b_ig=b_ig, lru_lambda=lru_lambda, lambda_q1=lambda_q1, lambda_k1=lambda_k1,
             lambda_q2=lambda_q2, lambda_k2=lambda_k2, subln_g=subln_g, w_out=w_out,
             norm_ffn=norm_ffn, w_gate=w_gate, w_up=w_up, w_down=w_down, norm_final=norm_final)
    y_prompt = _trunk(x_prompt, p, **_tiles(x_prompt.shape[1]))
    y_sample = _trunk(x_sample, p, **_tiles(x_sample.shape[1]))
    return (y_prompt, y_sample)
```

```python
import functools
import math

import jax
import jax.numpy as jnp
from jax import lax
from jax.experimental import pallas as pl
from jax.experimental.pallas import tpu as pltpu

F32 = jnp.float32
BF16 = jnp.bfloat16

N_HEADS = 4
HEAD_DV = 128
HEAD_DK = 64
N_LRU_BLOCKS = 8
CONV_W = 4
RG_C = 8.0
NORM_EPS = 1e-6
LOG2E = math.log2(math.e)
LANES = 128
SUBLANES = 8
VMEM_LIMIT = 56 * 1024 * 1024


def _rmsnorm(x, g):
    ms = jnp.mean(x * x, axis=-1, keepdims=True)
    return (x * lax.rsqrt(ms + NORM_EPS)) * g


def _inproj_kernel(x_ref, g_ref, w_ref, q_ref, k_ref, v_ref, xr_ref, gate_ref, *, qk_w, att_w, lru_w):
    h = _rmsnorm(x_ref[...], g_ref[...]).astype(BF16)

    def proj(lo, width):
        return jnp.dot(h, w_ref[:, lo:lo + width], preferred_element_type=F32)

    q_ref[...] = (proj(0, qk_w) * (LOG2E / math.sqrt(HEAD_DK))).astype(BF16)
    k_ref[...] = proj(qk_w, qk_w).astype(BF16)
    v_ref[...] = proj(2 * qk_w, att_w).astype(BF16)
    xr_ref[...] = proj(2 * qk_w + att_w, lru_w)
    gate_ref[...] = proj(2 * qk_w + att_w + lru_w, lru_w)


def _inproj(x2d, norm_g, w_in_bf16, *, tm):
    n_tok, d_model = x2d.shape
    qk_w = N_HEADS * 2 * HEAD_DK
    att_w = N_HEADS * HEAD_DV
    lru_w = (w_in_bf16.shape[1] - 2 * qk_w - att_w) // 2
    grid = (n_tok // tm,)
    row = lambda i: (i, 0)
    const = lambda i: (0, 0)
    return pl.pallas_call(
        functools.partial(_inproj_kernel, qk_w=qk_w, att_w=att_w, lru_w=lru_w),
        grid=grid,
        in_specs=[
            pl.BlockSpec((tm, d_model), row),
            pl.BlockSpec((1, d_model), const),
            pl.BlockSpec(w_in_bf16.shape, const, pipeline_mode=pl.Buffered(1)),
        ],
        out_specs=[
            pl.BlockSpec((tm, qk_w), row),
            pl.BlockSpec((tm, qk_w), row),
            pl.BlockSpec((tm, att_w), row),
            pl.BlockSpec((tm, lru_w), row),
            pl.BlockSpec((tm, lru_w), row),
        ],
        out_shape=[
            jax.ShapeDtypeStruct((n_tok, qk_w), BF16),
            jax.ShapeDtypeStruct((n_tok, qk_w), BF16),
            jax.ShapeDtypeStruct((n_tok, att_w), BF16),
            jax.ShapeDtypeStruct((n_tok, lru_w), F32),
            jax.ShapeDtypeStruct((n_tok, lru_w), F32),
        ],
        compiler_params=pltpu.CompilerParams(
            dimension_semantics=("arbitrary",), vmem_limit_bytes=VMEM_LIMIT),
        name="inproj",
    )(x2d, norm_g.reshape(1, d_model), w_in_bf16)


def _attn_kernel(slopes_ref, q_ref, k_ref, v_ref, lq1_ref, lk1_ref, lq2_ref, lk2_ref, g_ref,
                 o_ref, bias_ref, q2_ref, s_ref, p_ref, alpha_ref, m_ref, l_ref, acc_ref,
                 *, seq, tq, tk, lam_init):
    h = pl.program_id(0)
    b = pl.program_id(1)
    qi = pl.program_id(2)
    n_kc = seq // tk
    tab_w = 2 * seq - tq

    @pl.when((b == 0) & (qi == 0))
    def _():
        slope = slopes_ref[h]
        r = lax.broadcasted_iota(jnp.int32, (tq, tab_w), 0)
        u = lax.broadcasted_iota(jnp.int32, (tq, tab_w), 1)
        bias_ref[...] = (-LOG2E * slope) * jnp.abs(r - u + (seq - tq)).astype(F32)

    q = q_ref[0]
    lane = lax.broadcasted_iota(jnp.int32, q.shape, 1)
    zero = jnp.zeros_like(q)
    q2_ref[0:tq, :] = jnp.where(lane < HEAD_DK, q, zero)
    q2_ref[tq:2 * tq, :] = jnp.where(lane >= HEAD_DK, q, zero)
    col0 = seq - tq - qi * tq

    m_ref[...] = jnp.full(m_ref.shape, -jnp.inf, F32)
    l_ref[...] = jnp.zeros(l_ref.shape, F32)
    acc_ref[...] = jnp.zeros(acc_ref.shape, F32)

    def scores(c, slot):
        ks = pl.multiple_of(c * tk, tk)
        s = lax.dot_general(q2_ref[...], k_ref[0, pl.ds(ks, tk), :], (((1,), (1,)), ((), ())),
                            preferred_element_type=F32)
        bias = bias_ref[:, pl.ds(pl.multiple_of(col0 + ks, LANES), tk)]
        s_ref[slot] = s.reshape(2, tq, tk) + bias[None]

    def softmax(slot):
        tiles = [s_ref[slot, :, :, j * LANES:(j + 1) * LANES] for j in range(tk // LANES)]
        m_old = m_ref[...]
        m_new = jnp.maximum(m_old, jnp.max(functools.reduce(jnp.maximum, tiles), axis=-1, keepdims=True))
        alpha = jnp.exp2(m_old - m_new)
        l_new = alpha * l_ref[...]
        for j, s in enumerate(tiles):
            p = jnp.exp2(s - m_new)
            l_new = l_new + p
            p_ref[slot, :, :, j * LANES:(j + 1) * LANES] = p.astype(BF16)
        l_ref[...] = l_new
        m_ref[...] = m_new
        alpha_ref[slot] = alpha

    def values(c, slot):
        ks = pl.multiple_of(c * tk, tk)
        pv = jnp.dot(p_ref[slot].reshape(2 * tq, tk), v_ref[0, pl.ds(ks, tk), :],
                     preferred_element_type=F32)
        acc_ref[...] = alpha_ref[slot] * acc_ref[...] + pv.reshape(2, tq, HEAD_DV)

    def step(c, slot):
        values(c - 1, 1 - slot)
        softmax(slot)
        scores(c + 1, 1 - slot)

    scores(0, 0)
    if n_kc > 1:
        scores(1, 1)
    softmax(0)

    if n_kc > 2:
        assert n_kc % 2 == 0

        def two_steps(i, _):
            step(2 * i + 1, 1)
            step(2 * i + 2, 0)
            return 0

        lax.fori_loop(0, (n_kc - 2) // 2, two_steps, 0)
    if n_kc > 1:
        values(n_kc - 2, n_kc % 2)
        softmax((n_kc - 1) % 2)
    values(n_kc - 1, (n_kc - 1) % 2)

    lam = (jnp.exp(jnp.sum(lq1_ref[...] * lk1_ref[...], axis=-1, keepdims=True))
           - jnp.exp(jnp.sum(lq2_ref[...] * lk2_ref[...], axis=-1, keepdims=True)) + lam_init)
    l = jnp.sum(l_ref[...], axis=-1, keepdims=True)
    o = acc_ref[0] / l[0] - lam * (acc_ref[1] / l[1])
    o_ref[0] = (_rmsnorm(o, g_ref[...]) * (1.0 - lam_init)).astype(BF16)


def _attention(q, k, v, lq1, lk1, lq2, lk2, subln_g, *, lam_init, tq, tk):
    batch, seq, _ = q.shape
    slopes = jnp.asarray([2.0 ** (-8.0 * (h + 1) / N_HEADS) for h in range(N_HEADS)], F32)
    grid = (N_HEADS, batch, seq // tq)
    vec = lambda a: a.reshape(1, -1).astype(F32)
    small = lambda n: pl.BlockSpec((1, n), lambda h, b, i: (0, 0))
    return pl.pallas_call(
        functools.partial(_attn_kernel, seq=seq, tq=tq, tk=tk, lam_init=lam_init),
        grid=grid,
        in_specs=[
            pl.BlockSpec(memory_space=pltpu.SMEM),
            pl.BlockSpec((1, tq, HEAD_DV), lambda h, b, i: (b, i, h)),
            pl.BlockSpec((1, seq, HEAD_DV), lambda h, b, i: (b, 0, h)),
            pl.BlockSpec((1, seq, HEAD_DV), lambda h, b, i: (b, 0, h)),
            small(HEAD_DK), small(HEAD_DK), small(HEAD_DK), small(HEAD_DK), small(HEAD_DV),
        ],
        out_specs=pl.BlockSpec((1, tq, HEAD_DV), lambda h, b, i: (b, i, h)),
        out_shape=jax.ShapeDtypeStruct((batch, seq, N_HEADS * HEAD_DV), BF16),
        scratch_shapes=[
            pltpu.VMEM((tq, 2 * seq - tq), F32),
            pltpu.VMEM((2 * tq, HEAD_DV), BF16),
            pltpu.VMEM((2, 2, tq, tk), F32),
            pltpu.VMEM((2, 2, tq, tk), BF16),
            pltpu.VMEM((2, 2, tq, LANES), F32),
            pltpu.VMEM((2, tq, LANES), F32),
            pltpu.VMEM((2, tq, LANES), F32),
            pltpu.VMEM((2, tq, HEAD_DV), F32),
        ],
        compiler_params=pltpu.CompilerParams(
            dimension_semantics=("arbitrary", "arbitrary", "arbitrary"),
            vmem_limit_bytes=VMEM_LIMIT),
        name="diffattn",
    )(slopes, q, k, v, vec(lq1), vec(lk1), vec(lq2), vec(lk2), vec(subln_g))


def _sigmoid(x):
    return 1.0 / (1.0 + jnp.exp(-x))


def _softplus(x):
    return jnp.maximum(x, 0.0) + jnp.log1p(jnp.exp(-jnp.abs(x)))


def _lru_kernel(xr_ref, gate_ref, cw_ref, cb_ref, wg_ref, bg_ref, lam_ref, y_ref,
                xpad_ref, a_ref, b_ref, cin_ref, *, seq, tc):
    pad = SUBLANES

    xpad_ref[0:pad, :] = jnp.zeros((pad, LANES), F32)
    xpad_ref[pad + seq:pad + seq + pad, :] = jnp.zeros((pad, LANES), F32)
    xpad_ref[pad:pad + seq, :] = xr_ref[0]

    cw = cw_ref[...]
    cb = cb_ref[...]
    kk = _softplus(-lam_ref[0]) * (-0.5 * RG_C * LOG2E)

    def gates(c, _):
        t0 = pl.multiple_of(c * tc, tc)
        xc = cb
        for j in range(CONV_W):
            xc = xc + xpad_ref[pl.ds(t0 + pad - 2 + j, tc), :] * cw[j:j + 1, :]
        th = jnp.tanh(jnp.dot(xc.astype(BF16), wg_ref[0], preferred_element_type=F32) + bg_ref[0])
        xh = 0.5 * xc
        for d in range(2):
            t_r = th[:, (2 * d) * LANES:(2 * d + 1) * LANES]
            t_i = th[:, (2 * d + 1) * LANES:(2 * d + 2) * LANES]
            kd = kk[d:d + 1, :]
            a = jnp.exp2(kd * t_r + kd)
            m2 = jnp.maximum(1.0 - a * a, 1e-12)
            b = (m2 * lax.rsqrt(m2)) * (t_i * xh + xh)
            prod, loc = tile_scan(a.reshape(tile_shape), b.reshape(tile_shape), d == 1)
            a_ref[d, pl.ds(t0, tc), :] = prod.reshape(tc, LANES)
            b_ref[d, pl.ds(t0, tc), :] = loc.reshape(tc, LANES)
        return 0

    tile_shape = (tc // SUBLANES, SUBLANES, LANES)
    row = lax.broadcasted_iota(jnp.int32, tile_shape, 1)

    def tile_scan(a, b, reverse):
        edge = (SUBLANES - 1) if reverse else 0
        back = lambda x, d: pltpu.roll(x, (SUBLANES - d) if reverse else d, 1)
        am = jnp.where(row == edge, 0.0, a)
        p = jnp.where(row == edge, a, 0.0)
        for d in (1, 2, 4):
            b = b + am * back(b, d)
            p = p + am * back(p, d)
            if d < 4:
                am = am * back(am, d)
        return p, b

    lax.fori_loop(0, seq // tc, gates, 0)

    n_tiles = seq // SUBLANES

    def carry_scan(n, carry):
        out = []
        for d in range(2):
            v = n if d == 0 else n_tiles - 1 - n
            cin_ref[d, pl.ds(v, 1), :] = carry[d]
            last = v * SUBLANES + (SUBLANES - 1 if d == 0 else 0)
            out.append(b_ref[d, pl.ds(last, 1), :] + a_ref[d, pl.ds(last, 1), :] * carry[d])
        return tuple(out)

    zeros = jnp.zeros((1, LANES), F32)
    lax.fori_loop(0, n_tiles, carry_scan, (zeros, zeros), unroll=8)

    c1 = math.sqrt(2.0 / math.pi)

    def finish(c, _):
        t0 = pl.multiple_of(c * tc, tc)
        v0 = c * (tc // SUBLANES)
        hsum = None
        for d in range(2):
            cin = jnp.concatenate(
                [jnp.broadcast_to(cin_ref[d, pl.ds(v0 + i, 1), :], (SUBLANES, LANES))
                 for i in range(tc // SUBLANES)], axis=0)
            hd = b_ref[d, pl.ds(t0, tc), :] + a_ref[d, pl.ds(t0, tc), :] * cin
            hsum = hd if hsum is None else hsum + hd
        x = gate_ref[0, pl.ds(t0, tc), :]
        th = jnp.tanh(x * ((x * x) * (c1 * 0.044715) + c1))
        xh = 0.5 * x
        y_ref[0, pl.ds(t0, tc), :] = (hsum * (xh * th + xh)).astype(BF16)
        return 0

    lax.fori_loop(0, seq // tc, finish, 0)


def _lru_gate_weights(w_rg, b_rg, w_ig, b_ig):
    blk = w_rg.shape[-1]
    per = LANES // blk
    n_groups = N_LRU_BLOCKS // per

    def dense(w):
        w = w.reshape(n_groups, per, blk, blk)
        eye = jnp.eye(per, dtype=w.dtype)
        return jnp.einsum('gpcd,pq->gpcqd', w, eye).reshape(n_groups, LANES, LANES)

    wg = jnp.concatenate([dense(w_rg[0]), dense(w_ig[0]), dense(w_rg[1]), dense(w_ig[1])], axis=-1)
    bias = lambda v: v.reshape(n_groups, 1, LANES)
    bg = jnp.concatenate([bias(b_rg[0]), bias(b_ig[0]), bias(b_rg[1]), bias(b_ig[1])], axis=-1)
    return (0.5 * wg).astype(BF16), (0.5 * bg).astype(F32)


def _rglru(xr, gate, conv_w, conv_b, wg, bg, lru_lambda, *, tc):
    batch, seq, lru_w = xr.shape
    n_groups = lru_w // LANES
    lam = lru_lambda.reshape(2, n_groups, LANES).transpose(1, 0, 2)
    grid = (batch, n_groups)
    tok = pl.BlockSpec((1, seq, LANES), lambda b, j: (b, 0, j))
    return pl.pallas_call(
        functools.partial(_lru_kernel, seq=seq, tc=tc),
        grid=grid,
        in_specs=[
            tok, tok,
            pl.BlockSpec((CONV_W, LANES), lambda b, j: (0, j)),
            pl.BlockSpec((1, LANES), lambda b, j: (0, j)),
            pl.BlockSpec((1, LANES, 4 * LANES), lambda b, j: (j, 0, 0)),
            pl.BlockSpec((1, 1, 4 * LANES), lambda b, j: (j, 0, 0)),
            pl.BlockSpec((1, 2, LANES), lambda b, j: (j, 0, 0)),
        ],
        out_specs=tok,
        out_shape=jax.ShapeDtypeStruct((batch, seq, lru_w), BF16),
        scratch_shapes=[
            pltpu.VMEM((seq + 2 * SUBLANES, LANES), F32),
            pltpu.VMEM((2, seq, LANES), F32),
            pltpu.VMEM((2, seq, LANES), F32),
            pltpu.VMEM((2, seq // SUBLANES, LANES), F32),
        ],
        compiler_params=pltpu.CompilerParams(
            dimension_semantics=("arbitrary", "arbitrary"), vmem_limit_bytes=VMEM_LIMIT),
        name="rglru",
    )(xr, gate, conv_w, conv_b.reshape(1, lru_w), wg, bg, lam)


def _ffn_kernel(x_ref, o_ref, y_ref, wo_ref, g2_ref, wg_ref, wu_ref, wd_ref, g3_ref, out_ref, *, att_w):
    x1 = (x_ref[...]
          + jnp.dot(o_ref[...], wo_ref[0:att_w, :], preferred_element_type=F32)
          + jnp.dot(y_ref[...], wo_ref[att_w:, :], preferred_element_type=F32))
    h2 = _rmsnorm(x1, g2_ref[...]).astype(BF16)
    gt = jnp.dot(h2, wg_ref[...], preferred_element_type=F32)
    up = jnp.dot(h2, wu_ref[...], preferred_element_type=F32)
    act = ((gt * _sigmoid(gt)) * up).astype(BF16)
    x2 = x1 + jnp.dot(act, wd_ref[...], preferred_element_type=F32)
    out_ref[...] = _rmsnorm(x2, g3_ref[...])


def _out_ffn(x2d, o2d, y2d, w_out, norm_ffn, w_gate, w_up, w_down, norm_final, *, tm):
    n_tok, d_model = x2d.shape
    att_w = o2d.shape[1]
    lru_w = y2d.shape[1]
    row = lambda i: (i, 0)
    const = lambda i: (0, 0)
    resident = lambda a: pl.BlockSpec(a.shape, const, pipeline_mode=pl.Buffered(1))
    g2 = norm_ffn.reshape(1, d_model)
    g3 = norm_final.reshape(1, d_model)
    return pl.pallas_call(
        functools.partial(_ffn_kernel, att_w=att_w),
        grid=(n_tok // tm,),
        in_specs=[
            pl.BlockSpec((tm, d_model), row),
            pl.BlockSpec((tm, att_w), row),
            pl.BlockSpec((tm, lru_w), row),
            resident(w_out), resident(g2), resident(w_gate), resident(w_up), resident(w_down),
            resident(g3),
        ],
        out_specs=pl.BlockSpec((tm, d_model), row),
        out_shape=jax.ShapeDtypeStruct((n_tok, d_model), F32),
        compiler_params=pltpu.CompilerParams(
            dimension_semantics=("arbitrary",), vmem_limit_bytes=VMEM_LIMIT),
        name="out_ffn",
    )(x2d, o2d, y2d, w_out, g2, w_gate, w_up, w_down, g3)


def _trunk(x, p, *, tm_in, tq, tk, tc, tm_ffn):
    batch, seq, d_model = x.shape
    x2d = x.reshape(batch * seq, d_model)
    depth = p["w_in"].shape[0]
    for l in range(depth):
        last = l == depth - 1
        lam_init = 0.8 - 0.6 * math.exp(-0.3 * l)
        q, k, v, xr, gate = _inproj(x2d, p["norm_mix"][l], p["w_in"][l].astype(BF16), tm=tm_in)
        att_w = v.shape[1]
        lru_w = xr.shape[1]
        shp = lambda a: a.reshape(batch, seq, a.shape[1])
        o = _attention(shp(q), shp(k), shp(v), p["lambda_q1"][l], p["lambda_k1"][l],
                       p["lambda_q2"][l], p["lambda_k2"][l], p["subln_g"][l],
                       lam_init=lam_init, tq=tq, tk=tk)
        wg, bg = _lru_gate_weights(p["w_rg"][l], p["b_rg"][l], p["w_ig"][l], p["b_ig"][l])
        y = _rglru(shp(xr), shp(gate), p["conv_w"][l], p["conv_b"][l], wg, bg, p["lru_lambda"][l], tc=tc)
        assert last, "only the last layer's fused final norm is implemented"
        x2d = _out_ffn(x2d, o.reshape(batch * seq, att_w), y.reshape(batch * seq, lru_w),
                       p["w_out"][l].astype(BF16), p["norm_ffn"][l], p["w_gate"][l].astype(BF16),
                       p["w_up"][l].astype(BF16), p["w_down"][l].astype(BF16), p["norm_final"],
                       tm=tm_ffn)
    return x2d.reshape(batch, seq, d_model)


def _tiles(seq):
    return dict(tm_in=512, tq=min(512, seq), tk=min(512, seq), tc=min(256, seq // SUBLANES), tm_ffn=512)


def kernel(x_prompt, x_sample, norm_mix, w_in, conv_w, conv_b, w_rg, b_rg, w_ig, b_ig, lru_lambda, lambda_q1, lambda_k1, lambda_q2, lambda_k2, subln_g, w_out, norm_ffn, w_gate, w_up, w_down, norm_final):
    p = dict(norm_mix=norm_mix, w_in=w_in, conv_w=conv_w, conv_b=conv_b, w_rg=w_rg, b_rg=b_rg,
             w_ig=w_ig, b_ig=b_ig, lru_lambda=lru_lambda, lambda_q1=lambda_q1, lambda_k1=lambda_k1,
             lambda_q2=lambda_q2, lambda_k2=lambda_k2, subln_g=subln_g, w_out=w_out,
             norm_ffn=norm_ffn, w_gate=w_gate, w_up=w_up, w_down=w_down, norm_final=norm_final)
    y_prompt = _trunk(x_prompt, p, **_tiles(x_prompt.shape[1]))
    y_sample = _trunk(x_sample, p, **_tiles(x_sample.shape[1]))
    return (y_prompt, y_sample)
```

```python
import functools
import math

import jax
import jax.numpy as jnp
from jax import lax
from jax.experimental import pallas as pl
from jax.experimental.pallas import tpu as pltpu

F32 = jnp.float32
BF16 = jnp.bfloat16

N_HEADS = 4
HEAD_DV = 128
HEAD_DK = 64
N_LRU_BLOCKS = 8
CONV_W = 4
RG_C = 8.0
NORM_EPS = 1e-6
LOG2E = math.log2(math.e)
LANES = 128
SUBLANES = 8
VMEM_LIMIT = 56 * 1024 * 1024


def _rmsnorm(x, g):
    ms = jnp.mean(x * x, axis=-1, keepdims=True)
    return (x * lax.rsqrt(ms + NORM_EPS)) * g


def _inproj_kernel(x_ref, g_ref, w_ref, q_ref, k_ref, v_ref, xr_ref, gate_ref, *, qk_w, att_w, lru_w):
    h = _rmsnorm(x_ref[...], g_ref[...]).astype(BF16)

    def proj(lo, width):
        return jnp.dot(h, w_ref[:, lo:lo + width], preferred_element_type=F32)

    q_ref[...] = (proj(0, qk_w) * (LOG2E / math.sqrt(HEAD_DK))).astype(BF16)
    k_ref[...] = proj(qk_w, qk_w).astype(BF16)
    v_ref[...] = proj(2 * qk_w, att_w).astype(BF16)
    xr_ref[...] = proj(2 * qk_w + att_w, lru_w)
    gate_ref[...] = proj(2 * qk_w + att_w + lru_w, lru_w)


def _inproj(x2d, norm_g, w_in_bf16, *, tm):
    n_tok, d_model = x2d.shape
    qk_w = N_HEADS * 2 * HEAD_DK
    att_w = N_HEADS * HEAD_DV
    lru_w = (w_in_bf16.shape[1] - 2 * qk_w - att_w) // 2
    grid = (n_tok // tm,)
    row = lambda i: (i, 0)
    const = lambda i: (0, 0)
    return pl.pallas_call(
        functools.partial(_inproj_kernel, qk_w=qk_w, att_w=att_w, lru_w=lru_w),
        grid=grid,
        in_specs=[
            pl.BlockSpec((tm, d_model), row),
            pl.BlockSpec((1, d_model), const),
            pl.BlockSpec(w_in_bf16.shape, const, pipeline_mode=pl.Buffered(1)),
        ],
        out_specs=[
            pl.BlockSpec((tm, qk_w), row),
            pl.BlockSpec((tm, qk_w), row),
            pl.BlockSpec((tm, att_w), row),
            pl.BlockSpec((tm, lru_w), row),
            pl.BlockSpec((tm, lru_w), row),
        ],
        out_shape=[
            jax.ShapeDtypeStruct((n_tok, qk_w), BF16),
            jax.ShapeDtypeStruct((n_tok, qk_w), BF16),
            jax.ShapeDtypeStruct((n_tok, att_w), BF16),
            jax.ShapeDtypeStruct((n_tok, lru_w), F32),
            jax.ShapeDtypeStruct((n_tok, lru_w), F32),
        ],
        compiler_params=pltpu.CompilerParams(
            dimension_semantics=("arbitrary",), vmem_limit_bytes=VMEM_LIMIT),
        name="inproj",
    )(x2d, norm_g.reshape(1, d_model), w_in_bf16)


def _attn_kernel(slopes_ref, q_ref, k_ref, v_ref, lq1_ref, lk1_ref, lq2_ref, lk2_ref, g_ref,
                 o_ref, bias_ref, q2_ref, s_ref, p_ref, alpha_ref, m_ref, l_ref, acc_ref,
                 *, seq, tq, tk, lam_init):
    h = pl.program_id(0)
    b = pl.program_id(1)
    qi = pl.program_id(2)
    n_kc = seq // tk
    tab_w = 2 * seq - tq

    @pl.when((b == 0) & (qi == 0))
    def _():
        slope = slopes_ref[h]
        r = lax.broadcasted_iota(jnp.int32, (tq, tab_w), 0)
        u = lax.broadcasted_iota(jnp.int32, (tq, tab_w), 1)
        bias_ref[...] = (-LOG2E * slope) * jnp.abs(r - u + (seq - tq)).astype(F32)

    q = q_ref[0]
    lane = lax.broadcasted_iota(jnp.int32, q.shape, 1)
    zero = jnp.zeros_like(q)
    q2_ref[0:tq, :] = jnp.where(lane < HEAD_DK, q, zero)
    q2_ref[tq:2 * tq, :] = jnp.where(lane >= HEAD_DK, q, zero)
    col0 = seq - tq - qi * tq

    m_ref[...] = jnp.full(m_ref.shape, -jnp.inf, F32)
    l_ref[...] = jnp.zeros(l_ref.shape, F32)
    acc_ref[...] = jnp.zeros(acc_ref.shape, F32)

    def scores(c, slot):
        ks = pl.multiple_of(c * tk, tk)
        s = lax.dot_general(q2_ref[...], k_ref[0, pl.ds(ks, tk), :], (((1,), (1,)), ((), ())),
                            preferred_element_type=F32)
        bias = bias_ref[:, pl.ds(pl.multiple_of(col0 + ks, LANES), tk)]
        s_ref[slot] = s.reshape(2, tq, tk) + bias[None]

    def softmax(slot):
        tiles = [s_ref[slot, :, :, j * LANES:(j + 1) * LANES] for j in range(tk // LANES)]
        m_old = m_ref[...]
        m_new = jnp.maximum(m_old, jnp.max(functools.reduce(jnp.maximum, tiles), axis=-1, keepdims=True))
        alpha = jnp.exp2(m_old - m_new)
        l_new = alpha * l_ref[...]
        for j, s in enumerate(tiles):
            p = jnp.exp2(s - m_new)
            l_new = l_new + p
            p_ref[slot, :, :, j * LANES:(j + 1) * LANES] = p.astype(BF16)
        l_ref[...] = l_new
        m_ref[...] = m_new
        alpha_ref[slot] = alpha

    def values(c, slot):
        ks = pl.multiple_of(c * tk, tk)
        pv = jnp.dot(p_ref[slot].reshape(2 * tq, tk), v_ref[0, pl.ds(ks, tk), :],
                     preferred_element_type=F32)
        acc_ref[...] = alpha_ref[slot] * acc_ref[...] + pv.reshape(2, tq, HEAD_DV)

    def step(c, slot):
        values(c - 1, 1 - slot)
        softmax(slot)
        scores(c + 1, 1 - slot)

    scores(0, 0)
    if n_kc > 1:
        scores(1, 1)
    softmax(0)

    if n_kc > 2:
        assert n_kc % 2 == 0

        def two_steps(i, _):
            step(2 * i + 1, 1)
            step(2 * i + 2, 0)
            return 0

        lax.fori_loop(0, (n_kc - 2) // 2, two_steps, 0)
    if n_kc > 1:
        values(n_kc - 2, n_kc % 2)
        softmax((n_kc - 1) % 2)
    values(n_kc - 1, (n_kc - 1) % 2)

    lam = (jnp.exp(jnp.sum(lq1_ref[...] * lk1_ref[...], axis=-1, keepdims=True))
           - jnp.exp(jnp.sum(lq2_ref[...] * lk2_ref[...], axis=-1, keepdims=True)) + lam_init)
    l = jnp.sum(l_ref[...], axis=-1, keepdims=True)
    o = acc_ref[0] / l[0] - lam * (acc_ref[1] / l[1])
    o_ref[0] = (_rmsnorm(o, g_ref[...]) * (1.0 - lam_init)).astype(BF16)


def _attention(q, k, v, lq1, lk1, lq2, lk2, subln_g, *, lam_init, tq, tk):
    batch, seq, _ = q.shape
    slopes = jnp.asarray([2.0 ** (-8.0 * (h + 1) / N_HEADS) for h in range(N_HEADS)], F32)
    grid = (N_HEADS, batch, seq // tq)
    vec = lambda a: a.reshape(1, -1).astype(F32)
    small = lambda n: pl.BlockSpec((1, n), lambda h, b, i: (0, 0))
    return pl.pallas_call(
        functools.partial(_attn_kernel, seq=seq, tq=tq, tk=tk, lam_init=lam_init),
        grid=grid,
        in_specs=[
            pl.BlockSpec(memory_space=pltpu.SMEM),
            pl.BlockSpec((1, tq, HEAD_DV), lambda h, b, i: (b, i, h)),
            pl.BlockSpec((1, seq, HEAD_DV), lambda h, b, i: (b, 0, h)),
            pl.BlockSpec((1, seq, HEAD_DV), lambda h, b, i: (b, 0, h)),
            small(HEAD_DK), small(HEAD_DK), small(HEAD_DK), small(HEAD_DK), small(HEAD_DV),
        ],
        out_specs=pl.BlockSpec((1, tq, HEAD_DV), lambda h, b, i: (b, i, h)),
        out_shape=jax.ShapeDtypeStruct((batch, seq, N_HEADS * HEAD_DV), BF16),
        scratch_shapes=[
            pltpu.VMEM((tq, 2 * seq - tq), F32),
            pltpu.VMEM((2 * tq, HEAD_DV), BF16),
            pltpu.VMEM((2, 2, tq, tk), F32),
            pltpu.VMEM((2, 2, tq, tk), BF16),
            pltpu.VMEM((2, 2, tq, LANES), F32),
            pltpu.VMEM((2, tq, LANES), F32),
            pltpu.VMEM((2, tq, LANES), F32),
            pltpu.VMEM((2, tq, HEAD_DV), F32),
        ],
        compiler_params=pltpu.CompilerParams(
            dimension_semantics=("arbitrary", "arbitrary", "arbitrary"),
            vmem_limit_bytes=VMEM_LIMIT),
        name="diffattn",
    )(slopes, q, k, v, vec(lq1), vec(lk1), vec(lq2), vec(lk2), vec(subln_g))


def _sigmoid(x):
    return 1.0 / (1.0 + jnp.exp(-x))


def _softplus(x):
    return jnp.maximum(x, 0.0) + jnp.log1p(jnp.exp(-jnp.abs(x)))


def _lru_kernel(xr_ref, gate_ref, cw_ref, cb_ref, wg_ref, bg_ref, lam_ref, y_ref,
                xpad_ref, a_ref, b_ref, cin_ref, *, seq, tc):
    pad = SUBLANES

    xpad_ref[0:pad, :] = jnp.zeros((pad, LANES), F32)
    xpad_ref[pad + seq:pad + seq + pad, :] = jnp.zeros((pad, LANES), F32)
    xpad_ref[pad:pad + seq, :] = xr_ref[0]

    cw = cw_ref[...]
    cb = cb_ref[...]
    kk = _softplus(-lam_ref[0]) * (-0.5 * RG_C * LOG2E)

    def gates(c, _):
        t0 = pl.multiple_of(c * tc, tc)
        xc = cb
        for j in range(CONV_W):
            xc = xc + xpad_ref[pl.ds(t0 + pad - 2 + j, tc), :] * cw[j:j + 1, :]
        th = jnp.tanh(jnp.dot(xc.astype(BF16), wg_ref[0], preferred_element_type=F32) + bg_ref[0])
        xh = 0.5 * xc
        for d in range(2):
            t_r = th[:, (2 * d) * LANES:(2 * d + 1) * LANES]
            t_i = th[:, (2 * d + 1) * LANES:(2 * d + 2) * LANES]
            kd = kk[d:d + 1, :]
            a = jnp.exp2(kd * t_r + kd)
            m2 = jnp.maximum(1.0 - a * a, 1e-12)
            b = (m2 * lax.rsqrt(m2)) * (t_i * xh + xh)
            prod, loc = tile_scan(a.reshape(tile_shape), b.reshape(tile_shape), d == 1)
            a_ref[d, pl.ds(t0, tc), :] = prod.reshape(tc, LANES)
            b_ref[d, pl.ds(t0, tc), :] = loc.reshape(tc, LANES)
        return 0

    tile_shape = (tc // SUBLANES, SUBLANES, LANES)
    row = lax.broadcasted_iota(jnp.int32, tile_shape, 1)

    def tile_scan(a, b, reverse):
        back = lambda x, d: pltpu.roll(x, (SUBLANES - d) if reverse else d, 1)
        for d in (1, 2, 4):
            inside = (row < SUBLANES - d) if reverse else (row >= d)
            b = b + a * jnp.where(inside, back(b, d), 0.0)
            a = a * jnp.where(inside, back(a, d), 1.0)
        return a, b

    lax.fori_loop(0, seq // tc, gates, 0)

    n_tiles = seq // SUBLANES

    def carry_scan(n, carry):
        out = []
        for d in range(2):
            v = n if d == 0 else n_tiles - 1 - n
            cin_ref[d, pl.ds(v, 1), :] = carry[d]
            last = v * SUBLANES + (SUBLANES - 1 if d == 0 else 0)
            out.append(b_ref[d, pl.ds(last, 1), :] + a_ref[d, pl.ds(last, 1), :] * carry[d])
        return tuple(out)

    zeros = jnp.zeros((1, LANES), F32)
    lax.fori_loop(0, n_tiles, carry_scan, (zeros, zeros), unroll=8)

    c1 = math.sqrt(2.0 / math.pi)

    def finish(c, _):
        t0 = pl.multiple_of(c * tc, tc)
        v0 = c * (tc // SUBLANES)
        hsum = None
        for d in range(2):
            cin = jnp.concatenate(
                [jnp.broadcast_to(cin_ref[d, pl.ds(v0 + i, 1), :], (SUBLANES, LANES))
                 for i in range(tc // SUBLANES)], axis=0)
            hd = b_ref[d, pl.ds(t0, tc), :] + a_ref[d, pl.ds(t0, tc), :] * cin
            hsum = hd if hsum is None else hsum + hd
        x = gate_ref[0, pl.ds(t0, tc), :]
        th = jnp.tanh(x * ((x * x) * (c1 * 0.044715) + c1))
        xh = 0.5 * x
        y_ref[0, pl.ds(t0, tc), :] = (hsum * (xh * th + xh)).astype(BF16)
        return 0

    lax.fori_loop(0, seq // tc, finish, 0)


def _lru_gate_weights(w_rg, b_rg, w_ig, b_ig):
    blk = w_rg.shape[-1]
    per = LANES // blk
    n_groups = N_LRU_BLOCKS // per

    def dense(w):
        w = w.reshape(n_groups, per, blk, blk)
        eye = jnp.eye(per, dtype=w.dtype)
        return jnp.einsum('gpcd,pq->gpcqd', w, eye).reshape(n_groups, LANES, LANES)

    wg = jnp.concatenate([dense(w_rg[0]), dense(w_ig[0]), dense(w_rg[1]), dense(w_ig[1])], axis=-1)
    bias = lambda v: v.reshape(n_groups, 1, LANES)
    bg = jnp.concatenate([bias(b_rg[0]), bias(b_ig[0]), bias(b_rg[1]), bias(b_ig[1])], axis=-1)
    return (0.5 * wg).astype(BF16), (0.5 * bg).astype(F32)


def _rglru(xr, gate, conv_w, conv_b, wg, bg, lru_lambda, *, tc):
    batch, seq, lru_w = xr.shape
    n_groups = lru_w // LANES
    lam = lru_lambda.reshape(2, n_groups, LANES).transpose(1, 0, 2)
    grid = (batch, n_groups)
    tok = pl.BlockSpec((1, seq, LANES), lambda b, j: (b, 0, j))
    return pl.pallas_call(
        functools.partial(_lru_kernel, seq=seq, tc=tc),
        grid=grid,
        in_specs=[
            tok, tok,
            pl.BlockSpec((CONV_W, LANES), lambda b, j: (0, j)),
            pl.BlockSpec((1, LANES), lambda b, j: (0, j)),
            pl.BlockSpec((1, LANES, 4 * LANES), lambda b, j: (j, 0, 0)),
            pl.BlockSpec((1, 1, 4 * LANES), lambda b, j: (j, 0, 0)),
            pl.BlockSpec((1, 2, LANES), lambda b, j: (j, 0, 0)),
        ],
        out_specs=tok,
        out_shape=jax.ShapeDtypeStruct((batch, seq, lru_w), BF16),
        scratch_shapes=[
            pltpu.VMEM((seq + 2 * SUBLANES, LANES), F32),
            pltpu.VMEM((2, seq, LANES), F32),
            pltpu.VMEM((2, seq, LANES), F32),
            pltpu.VMEM((2, seq // SUBLANES, LANES), F32),
        ],
        compiler_params=pltpu.CompilerParams(
            dimension_semantics=("arbitrary", "arbitrary"), vmem_limit_bytes=VMEM_LIMIT),
        name="rglru",
    )(xr, gate, conv_w, conv_b.reshape(1, lru_w), wg, bg, lam)


def _ffn_kernel(x_ref, o_ref, y_ref, wo_ref, g2_ref, wg_ref, wu_ref, wd_ref, g3_ref, out_ref, *, att_w):
    x1 = (x_ref[...]
          + jnp.dot(o_ref[...], wo_ref[0:att_w, :], preferred_element_type=F32)
          + jnp.dot(y_ref[...], wo_ref[att_w:, :], preferred_element_type=F32))
    h2 = _rmsnorm(x1, g2_ref[...]).astype(BF16)
    gt = jnp.dot(h2, wg_ref[...], preferred_element_type=F32)
    up = jnp.dot(h2, wu_ref[...], preferred_element_type=F32)
    act = ((gt * _sigmoid(gt)) * up).astype(BF16)
    x2 = x1 + jnp.dot(act, wd_ref[...], preferred_element_type=F32)
    out_ref[...] = _rmsnorm(x2, g3_ref[...])


def _out_ffn(x2d, o2d, y2d, w_out, norm_ffn, w_gate, w_up, w_down, norm_final, *, tm):
    n_tok, d_model = x2d.shape
    att_w = o2d.shape[1]
    lru_w = y2d.shape[1]
    row = lambda i: (i, 0)
    const = lambda i: (0, 0)
    resident = lambda a: pl.BlockSpec(a.shape, const, pipeline_mode=pl.Buffered(1))
    g2 = norm_ffn.reshape(1, d_model)
    g3 = norm_final.reshape(1, d_model)
    return pl.pallas_call(
        functools.partial(_ffn_kernel, att_w=att_w),
        grid=(n_tok // tm,),
        in_specs=[
            pl.BlockSpec((tm, d_model), row),
            pl.BlockSpec((tm, att_w), row),
            pl.BlockSpec((tm, lru_w), row),
            resident(w_out), resident(g2), resident(w_gate), resident(w_up), resident(w_down),
            resident(g3),
        ],
        out_specs=pl.BlockSpec((tm, d_model), row),
        out_shape=jax.ShapeDtypeStruct((n_tok, d_model), F32),
        compiler_params=pltpu.CompilerParams(
            dimension_semantics=("arbitrary",), vmem_limit_bytes=VMEM_LIMIT),
        name="out_ffn",
    )(x2d, o2d, y2d, w_out, g2, w_gate, w_up, w_down, g3)


def _trunk(x, p, *, tm_in, tq, tk, tc, tm_ffn):
    batch, seq, d_model = x.shape
    x2d = x.reshape(batch * seq, d_model)
    depth = p["w_in"].shape[0]
    for l in range(depth):
        last = l == depth - 1
        lam_init = 0.8 - 0.6 * math.exp(-0.3 * l)
        q, k, v, xr, gate = _inproj(x2d, p["norm_mix"][l], p["w_in"][l].astype(BF16), tm=tm_in)
        att_w = v.shape[1]
        lru_w = xr.shape[1]
        shp = lambda a: a.reshape(batch, seq, a.shape[1])
        o = _attention(shp(q), shp(k), shp(v), p["lambda_q1"][l], p["lambda_k1"][l],
                       p["lambda_q2"][l], p["lambda_k2"][l], p["subln_g"][l],
                       lam_init=lam_init, tq=tq, tk=tk)
        wg, bg = _lru_gate_weights(p["w_rg"][l], p["b_rg"][l], p["w_ig"][l], p["b_ig"][l])
        y = _rglru(shp(xr), shp(gate), p["conv_w"][l], p["conv_b"][l], wg, bg, p["lru_lambda"][l], tc=tc)
        assert last, "only the last layer's fused final norm is implemented"
        x2d = _out_ffn(x2d, o.reshape(batch * seq, att_w), y.reshape(batch * seq, lru_w),
                       p["w_out"][l].astype(BF16), p["norm_ffn"][l], p["w_gate"][l].astype(BF16),
                       p["w_up"][l].astype(BF16), p["w_down"][l].astype(BF16), p["norm_final"],
                       tm=tm_ffn)
    return x2d.reshape(batch, seq, d_model)


def _tiles(seq):
    return dict(tm_in=512, tq=min(1024, seq), tk=min(512, seq), tc=min(256, seq // SUBLANES), tm_ffn=512)


def kernel(x_prompt, x_sample, norm_mix, w_in, conv_w, conv_b, w_rg, b_rg, w_ig, b_ig, lru_lambda, lambda_q1, lambda_k1, lambda_q2, lambda_k2, subln_g, w_out, norm_ffn, w_gate, w_up, w_down, norm_final):
    p = dict(norm_mix=norm_mix, w_in=w_in, conv_w=conv_w, conv_b=conv_b, w_rg=w_rg, b_rg=b_rg,
             w_ig=w_ig, b_ig=b_ig, lru_lambda=lru_lambda, lambda_q1=lambda_q1, lambda_k1=lambda_k1,
             lambda_q2=lambda_q2, lambda_k2=lambda_k2, subln_g=subln_g, w_out=w_out,
             norm_ffn=norm_ffn, w_gate=w_gate, w_up=w_up, w_down=w_down, norm_final=norm_final)
    y_prompt = _trunk(x_prompt, p, **_tiles(x_prompt.shape[1]))
    y_sample = _trunk(x_sample, p, **_tiles(x_sample.shape[1]))
    return (y_prompt, y_sample)
```

```python
import functools
import math

import jax
import jax.numpy as jnp
from jax import lax
from jax.experimental import pallas as pl
from jax.experimental.pallas import tpu as pltpu

F32 = jnp.float32
BF16 = jnp.bfloat16

N_HEADS = 4
HEAD_DV = 128
HEAD_DK = 64
N_LRU_BLOCKS = 8
CONV_W = 4
RG_C = 8.0
NORM_EPS = 1e-6
LOG2E = math.log2(math.e)
LANES = 128
SUBLANES = 8
VMEM_LIMIT = 56 * 1024 * 1024


def _rmsnorm(x, g):
    ms = jnp.mean(x * x, axis=-1, keepdims=True)
    return (x * lax.rsqrt(ms + NORM_EPS)) * g


def _inproj_kernel(x_ref, g_ref, w_ref, q_ref, k_ref, v_ref, xr_ref, gate_ref, *, qk_w, att_w, lru_w):
    h = _rmsnorm(x_ref[...], g_ref[...]).astype(BF16)

    def proj(lo, width):
        return jnp.dot(h, w_ref[:, lo:lo + width], preferred_element_type=F32)

    q_ref[...] = (proj(0, qk_w) * (LOG2E / math.sqrt(HEAD_DK))).astype(BF16)
    k_ref[...] = proj(qk_w, qk_w).astype(BF16)
    v_ref[...] = proj(2 * qk_w, att_w).astype(BF16)
    xr_ref[...] = proj(2 * qk_w + att_w, lru_w)
    gate_ref[...] = proj(2 * qk_w + att_w + lru_w, lru_w)


def _inproj(x2d, norm_g, w_in_bf16, *, tm):
    n_tok, d_model = x2d.shape
    qk_w = N_HEADS * 2 * HEAD_DK
    att_w = N_HEADS * HEAD_DV
    lru_w = (w_in_bf16.shape[1] - 2 * qk_w - att_w) // 2
    grid = (n_tok // tm,)
    row = lambda i: (i, 0)
    const = lambda i: (0, 0)
    return pl.pallas_call(
        functools.partial(_inproj_kernel, qk_w=qk_w, att_w=att_w, lru_w=lru_w),
        grid=grid,
        in_specs=[
            pl.BlockSpec((tm, d_model), row),
            pl.BlockSpec((1, d_model), const),
            pl.BlockSpec(w_in_bf16.shape, const, pipeline_mode=pl.Buffered(1)),
        ],
        out_specs=[
            pl.BlockSpec((tm, qk_w), row),
            pl.BlockSpec((tm, qk_w), row),
            pl.BlockSpec((tm, att_w), row),
            pl.BlockSpec((tm, lru_w), row),
            pl.BlockSpec((tm, lru_w), row),
        ],
        out_shape=[
            jax.ShapeDtypeStruct((n_tok, qk_w), BF16),
            jax.ShapeDtypeStruct((n_tok, qk_w), BF16),
            jax.ShapeDtypeStruct((n_tok, att_w), BF16),
            jax.ShapeDtypeStruct((n_tok, lru_w), F32),
            jax.ShapeDtypeStruct((n_tok, lru_w), F32),
        ],
        compiler_params=pltpu.CompilerParams(
            dimension_semantics=("arbitrary",), vmem_limit_bytes=VMEM_LIMIT),
        name="inproj",
    )(x2d, norm_g.reshape(1, d_model), w_in_bf16)


def _attn_kernel(slopes_ref, q_ref, k_ref, v_ref, lq1_ref, lk1_ref, lq2_ref, lk2_ref, g_ref,
                 o_ref, bias_ref, q2_ref, s_ref, p_ref, alpha_ref, m_ref, l_ref, acc_ref,
                 *, seq, tq, tk, lam_init):
    h = pl.program_id(0)
    b = pl.program_id(1)
    qi = pl.program_id(2)
    n_kc = seq // tk
    tab_w = 2 * seq - tq

    @pl.when((b == 0) & (qi == 0))
    def _():
        slope = slopes_ref[h]
        r = lax.broadcasted_iota(jnp.int32, (tq, tab_w), 0)
        u = lax.broadcasted_iota(jnp.int32, (tq, tab_w), 1)
        bias_ref[...] = (-LOG2E * slope) * jnp.abs(r - u + (seq - tq)).astype(F32)

    q = q_ref[0]
    lane = lax.broadcasted_iota(jnp.int32, q.shape, 1)
    zero = jnp.zeros_like(q)
    q2_ref[0:tq, :] = jnp.where(lane < HEAD_DK, q, zero)
    q2_ref[tq:2 * tq, :] = jnp.where(lane >= HEAD_DK, q, zero)
    col0 = seq - tq - qi * tq

    m_ref[...] = jnp.full(m_ref.shape, -jnp.inf, F32)
    l_ref[...] = jnp.zeros(l_ref.shape, F32)
    acc_ref[...] = jnp.zeros(acc_ref.shape, F32)

    def scores(c, slot):
        ks = pl.multiple_of(c * tk, tk)
        s = lax.dot_general(q2_ref[...], k_ref[0, pl.ds(ks, tk), :], (((1,), (1,)), ((), ())),
                            preferred_element_type=F32)
        bias = bias_ref[:, pl.ds(pl.multiple_of(col0 + ks, LANES), tk)]
        s_ref[slot] = s.reshape(2, tq, tk) + bias[None]

    def softmax(slot):
        tiles = [s_ref[slot, :, :, j * LANES:(j + 1) * LANES] for j in range(tk // LANES)]
        m_old = m_ref[...]
        m_new = jnp.maximum(m_old, jnp.max(functools.reduce(jnp.maximum, tiles), axis=-1, keepdims=True))
        alpha = jnp.exp2(m_old - m_new)
        l_new = alpha * l_ref[...]
        for j, s in enumerate(tiles):
            p = jnp.exp2(s - m_new)
            l_new = l_new + p
            p_ref[slot, :, :, j * LANES:(j + 1) * LANES] = p.astype(BF16)
        l_ref[...] = l_new
        m_ref[...] = m_new
        alpha_ref[slot] = alpha

    def values(c, slot):
        ks = pl.multiple_of(c * tk, tk)
        pv = jnp.dot(p_ref[slot].reshape(2 * tq, tk), v_ref[0, pl.ds(ks, tk), :],
                     preferred_element_type=F32)
        acc_ref[...] = alpha_ref[slot] * acc_ref[...] + pv.reshape(2, tq, HEAD_DV)

    def step(c, slot):
        values(c - 1, 1 - slot)
        softmax(slot)
        scores(c + 1, 1 - slot)

    scores(0, 0)
    if n_kc > 1:
        scores(1, 1)
    softmax(0)

    if n_kc > 2:
        assert n_kc % 2 == 0

        def two_steps(i, _):
            step(2 * i + 1, 1)
            step(2 * i + 2, 0)
            return 0

        lax.fori_loop(0, (n_kc - 2) // 2, two_steps, 0)
    if n_kc > 1:
        values(n_kc - 2, n_kc % 2)
        softmax((n_kc - 1) % 2)
    values(n_kc - 1, (n_kc - 1) % 2)

    lam = (jnp.exp(jnp.sum(lq1_ref[...] * lk1_ref[...], axis=-1, keepdims=True))
           - jnp.exp(jnp.sum(lq2_ref[...] * lk2_ref[...], axis=-1, keepdims=True)) + lam_init)
    l = jnp.sum(l_ref[...], axis=-1, keepdims=True)
    o = acc_ref[0] / l[0] - lam * (acc_ref[1] / l[1])
    o_ref[0] = (_rmsnorm(o, g_ref[...]) * (1.0 - lam_init)).astype(BF16)


def _attention(q, k, v, lq1, lk1, lq2, lk2, subln_g, *, lam_init, tq, tk):
    batch, seq, _ = q.shape
    slopes = jnp.asarray([2.0 ** (-8.0 * (h + 1) / N_HEADS) for h in range(N_HEADS)], F32)
    grid = (N_HEADS, batch, seq // tq)
    vec = lambda a: a.reshape(1, -1).astype(F32)
    small = lambda n: pl.BlockSpec((1, n), lambda h, b, i: (0, 0))
    return pl.pallas_call(
        functools.partial(_attn_kernel, seq=seq, tq=tq, tk=tk, lam_init=lam_init),
        grid=grid,
        in_specs=[
            pl.BlockSpec(memory_space=pltpu.SMEM),
            pl.BlockSpec((1, tq, HEAD_DV), lambda h, b, i: (b, i, h)),
            pl.BlockSpec((1, seq, HEAD_DV), lambda h, b, i: (b, 0, h)),
            pl.BlockSpec((1, seq, HEAD_DV), lambda h, b, i: (b, 0, h)),
            small(HEAD_DK), small(HEAD_DK), small(HEAD_DK), small(HEAD_DK), small(HEAD_DV),
        ],
        out_specs=pl.BlockSpec((1, tq, HEAD_DV), lambda h, b, i: (b, i, h)),
        out_shape=jax.ShapeDtypeStruct((batch, seq, N_HEADS * HEAD_DV), BF16),
        scratch_shapes=[
            pltpu.VMEM((tq, 2 * seq - tq), F32),
            pltpu.VMEM((2 * tq, HEAD_DV), BF16),
            pltpu.VMEM((2, 2, tq, tk), F32),
            pltpu.VMEM((2, 2, tq, tk), BF16),
            pltpu.VMEM((2, 2, tq, LANES), F32),
            pltpu.VMEM((2, tq, LANES), F32),
            pltpu.VMEM((2, tq, LANES), F32),
            pltpu.VMEM((2, tq, HEAD_DV), F32),
        ],
        compiler_params=pltpu.CompilerParams(
            dimension_semantics=("arbitrary", "arbitrary", "arbitrary"),
            vmem_limit_bytes=VMEM_LIMIT),
        name="diffattn",
    )(slopes, q, k, v, vec(lq1), vec(lk1), vec(lq2), vec(lk2), vec(subln_g))


def _sigmoid(x):
    return 1.0 / (1.0 + jnp.exp(-x))


def _softplus(x):
    return jnp.maximum(x, 0.0) + jnp.log1p(jnp.exp(-jnp.abs(x)))


def _lru_kernel(xr_ref, gate_ref, cw_ref, cb_ref, wg_ref, bg_ref, lam_ref, y_ref,
                xpad_ref, a_ref, b_ref, h_ref, *, seq, tc, n_grp):
    pad = SUBLANES

    xpad_ref[0:pad, :] = jnp.zeros((pad, LANES), F32)
    xpad_ref[pad + seq:pad + seq + pad, :] = jnp.zeros((pad, LANES), F32)
    xpad_ref[pad:pad + seq, :] = xr_ref[0]

    cw = cw_ref[...]
    cb = cb_ref[...]
    kk = _softplus(-lam_ref[0]) * (-0.5 * RG_C * LOG2E)

    def gates(c, _):
        t0 = pl.multiple_of(c * tc, tc)
        xc = cb
        for j in range(CONV_W):
            xc = xc + xpad_ref[pl.ds(t0 + pad - 2 + j, tc), :] * cw[j:j + 1, :]
        th = jnp.tanh(jnp.dot(xc.astype(BF16), wg_ref[0], preferred_element_type=F32) + bg_ref[0])
        xh = 0.5 * xc
        for d in range(2):
            t_r = th[:, (2 * d) * LANES:(2 * d + 1) * LANES]
            t_i = th[:, (2 * d + 1) * LANES:(2 * d + 2) * LANES]
            kd = kk[d:d + 1, :]
            a = jnp.exp2(kd * t_r + kd)
            m2 = jnp.maximum(1.0 - a * a, 1e-12)
            b = (m2 * lax.rsqrt(m2)) * (t_i * xh + xh)
            a_ref[d, pl.ds(t0, tc), :] = a
            b_ref[d, pl.ds(t0, tc), :] = b
        return 0

    part = seq // SUBLANES + 4
    assert (part // 4) % 2 == 1 and part % n_grp == 0
    seg = part // n_grp
    n_pad = SUBLANES * part - seq
    for d in range(2):
        a_ref[d, seq:seq + n_pad, :] = jnp.ones((n_pad, LANES), F32)
        b_ref[d, seq:seq + n_pad, :] = jnp.zeros((n_pad, LANES), F32)

    def tile(g, t):
        return pl.ds(g * seg + t, SUBLANES, stride=part)

    lax.fori_loop(0, seq // tc, gates, 0, unroll=2)

    zeros = jnp.zeros((SUBLANES, LANES), F32)
    ones = jnp.ones((SUBLANES, LANES), F32)

    def pass1(n, carry):
        out = []
        for d in range(2):
            t = n if d == 0 else seg - 1 - n
            hs, ps = carry[d]
            nh, npr = [], []
            for g in range(n_grp):
                a = a_ref[d, tile(g, t), :]
                nh.append(a * hs[g] + b_ref[d, tile(g, t), :])
                npr.append(a * ps[g])
            out.append((tuple(nh), tuple(npr)))
        return tuple(out)

    ends = lax.fori_loop(0, seg, pass1,
                         tuple(((zeros,) * n_grp, (ones,) * n_grp) for _ in range(2)), unroll=2)

    def chain_rows(e, q, reverse):
        row = lax.broadcasted_iota(jnp.int32, (SUBLANES, LANES), 0)
        edge = (SUBLANES - 1) if reverse else 0
        shift = (SUBLANES - 1) if reverse else 1
        carry = zeros
        for _ in range(SUBLANES - 1):
            carry = jnp.where(row == edge, 0.0, pltpu.roll(e + q * carry, shift, 0))
        return carry

    def entry_states(h_end, p_end, reverse):
        order = range(n_grp - 1, -1, -1) if reverse else range(n_grp)
        local, prod = {}, {}
        c, q = zeros, ones
        for g in order:
            local[g], prod[g] = c, q
            c = h_end[g] + p_end[g] * c
            q = p_end[g] * q
        row_in = chain_rows(c, q, reverse)
        return tuple(local[g] + prod[g] * row_in for g in range(n_grp))

    def pass2(n, carry):
        out = []
        for d in range(2):
            t = n if d == 0 else seg - 1 - n
            nh = []
            for g in range(n_grp):
                hcur = a_ref[d, tile(g, t), :] * carry[d][g] + b_ref[d, tile(g, t), :]
                h_ref[d, tile(g, t), :] = hcur
                nh.append(hcur)
            out.append(tuple(nh))
        return tuple(out)

    lax.fori_loop(0, seg, pass2,
                  tuple(entry_states(ends[d][0], ends[d][1], d == 1) for d in range(2)), unroll=2)

    c1 = math.sqrt(2.0 / math.pi)

    def finish(c, _):
        t0 = pl.multiple_of(c * tc, tc)
        hsum = h_ref[0, pl.ds(t0, tc), :] + h_ref[1, pl.ds(t0, tc), :]
        x = gate_ref[0, pl.ds(t0, tc), :]
        th = jnp.tanh(x * ((x * x) * (c1 * 0.044715) + c1))
        xh = 0.5 * x
        y_ref[0, pl.ds(t0, tc), :] = (hsum * (xh * th + xh)).astype(BF16)
        return 0

    lax.fori_loop(0, seq // tc, finish, 0)


def _lru_gate_weights(w_rg, b_rg, w_ig, b_ig):
    blk = w_rg.shape[-1]
    per = LANES // blk
    n_groups = N_LRU_BLOCKS // per

    def dense(w):
        w = w.reshape(n_groups, per, blk, blk)
        eye = jnp.eye(per, dtype=w.dtype)
        return jnp.einsum('gpcd,pq->gpcqd', w, eye).reshape(n_groups, LANES, LANES)

    wg = jnp.concatenate([dense(w_rg[0]), dense(w_ig[0]), dense(w_rg[1]), dense(w_ig[1])], axis=-1)
    bias = lambda v: v.reshape(n_groups, 1, LANES)
    bg = jnp.concatenate([bias(b_rg[0]), bias(b_ig[0]), bias(b_rg[1]), bias(b_ig[1])], axis=-1)
    return (0.5 * wg).astype(BF16), (0.5 * bg).astype(F32)


def _rglru(xr, gate, conv_w, conv_b, wg, bg, lru_lambda, *, tc, n_grp):
    batch, seq, lru_w = xr.shape
    n_groups = lru_w // LANES
    lam = lru_lambda.reshape(2, n_groups, LANES).transpose(1, 0, 2)
    grid = (batch, n_groups)
    tok = pl.BlockSpec((1, seq, LANES), lambda b, j: (b, 0, j))
    return pl.pallas_call(
        functools.partial(_lru_kernel, seq=seq, tc=tc, n_grp=n_grp),
        grid=grid,
        in_specs=[
            tok, tok,
            pl.BlockSpec((CONV_W, LANES), lambda b, j: (0, j)),
            pl.BlockSpec((1, LANES), lambda b, j: (0, j)),
            pl.BlockSpec((1, LANES, 4 * LANES), lambda b, j: (j, 0, 0)),
            pl.BlockSpec((1, 1, 4 * LANES), lambda b, j: (j, 0, 0)),
            pl.BlockSpec((1, 2, LANES), lambda b, j: (j, 0, 0)),
        ],
        out_specs=tok,
        out_shape=jax.ShapeDtypeStruct((batch, seq, lru_w), BF16),
        scratch_shapes=[
            pltpu.VMEM((seq + 2 * SUBLANES, LANES), F32),
            pltpu.VMEM((2, seq + 4 * SUBLANES, LANES), F32),
            pltpu.VMEM((2, seq + 4 * SUBLANES, LANES), F32),
            pltpu.VMEM((2, seq + 4 * SUBLANES, LANES), F32),
        ],
        compiler_params=pltpu.CompilerParams(
            dimension_semantics=("arbitrary", "arbitrary"), vmem_limit_bytes=VMEM_LIMIT),
        name="rglru",
    )(xr, gate, conv_w, conv_b.reshape(1, lru_w), wg, bg, lam)


def _ffn_kernel(x_ref, o_ref, y_ref, wo_ref, g2_ref, wg_ref, wu_ref, wd_ref, g3_ref, out_ref, *, att_w):
    x1 = (x_ref[...]
          + jnp.dot(o_ref[...], wo_ref[0:att_w, :], preferred_element_type=F32)
          + jnp.dot(y_ref[...], wo_ref[att_w:, :], preferred_element_type=F32))
    h2 = _rmsnorm(x1, g2_ref[...]).astype(BF16)
    gt = jnp.dot(h2, wg_ref[...], preferred_element_type=F32)
    up = jnp.dot(h2, wu_ref[...], preferred_element_type=F32)
    act = ((gt * _sigmoid(gt)) * up).astype(BF16)
    x2 = x1 + jnp.dot(act, wd_ref[...], preferred_element_type=F32)
    out_ref[...] = _rmsnorm(x2, g3_ref[...])


def _out_ffn(x2d, o2d, y2d, w_out, norm_ffn, w_gate, w_up, w_down, norm_final, *, tm):
    n_tok, d_model = x2d.shape
    att_w = o2d.shape[1]
    lru_w = y2d.shape[1]
    row = lambda i: (i, 0)
    const = lambda i: (0, 0)
    resident = lambda a: pl.BlockSpec(a.shape, const, pipeline_mode=pl.Buffered(1))
    g2 = norm_ffn.reshape(1, d_model)
    g3 = norm_final.reshape(1, d_model)
    return pl.pallas_call(
        functools.partial(_ffn_kernel, att_w=att_w),
        grid=(n_tok // tm,),
        in_specs=[
            pl.BlockSpec((tm, d_model), row),
            pl.BlockSpec((tm, att_w), row),
            pl.BlockSpec((tm, lru_w), row),
            resident(w_out), resident(g2), resident(w_gate), resident(w_up), resident(w_down),
            resident(g3),
        ],
        out_specs=pl.BlockSpec((tm, d_model), row),
        out_shape=jax.ShapeDtypeStruct((n_tok, d_model), F32),
        compiler_params=pltpu.CompilerParams(
            dimension_semantics=("arbitrary",), vmem_limit_bytes=VMEM_LIMIT),
        name="out_ffn",
    )(x2d, o2d, y2d, w_out, g2, w_gate, w_up, w_down, g3)


def _trunk(x, p, *, tm_in, tq, tk, tc, n_grp, tm_ffn):
    batch, seq, d_model = x.shape
    x2d = x.reshape(batch * seq, d_model)
    depth = p["w_in"].shape[0]
    for l in range(depth):
        last = l == depth - 1
        lam_init = 0.8 - 0.6 * math.exp(-0.3 * l)
        q, k, v, xr, gate = _inproj(x2d, p["norm_mix"][l], p["w_in"][l].astype(BF16), tm=tm_in)
        att_w = v.shape[1]
        lru_w = xr.shape[1]
        shp = lambda a: a.reshape(batch, seq, a.shape[1])
        o = _attention(shp(q), shp(k), shp(v), p["lambda_q1"][l], p["lambda_k1"][l],
                       p["lambda_q2"][l], p["lambda_k2"][l], p["subln_g"][l],
                       lam_init=lam_init, tq=tq, tk=tk)
        wg, bg = _lru_gate_weights(p["w_rg"][l], p["b_rg"][l], p["w_ig"][l], p["b_ig"][l])
        y = _rglru(shp(xr), shp(gate), p["conv_w"][l], p["conv_b"][l], wg, bg, p["lru_lambda"][l],
                   tc=tc, n_grp=n_grp)
        assert last, "only the last layer's fused final norm is implemented"
        x2d = _out_ffn(x2d, o.reshape(batch * seq, att_w), y.reshape(batch * seq, lru_w),
                       p["w_out"][l].astype(BF16), p["norm_ffn"][l], p["w_gate"][l].astype(BF16),
                       p["w_up"][l].astype(BF16), p["w_down"][l].astype(BF16), p["norm_final"],
                       tm=tm_ffn)
    return x2d.reshape(batch, seq, d_model)


def _tiles(seq):
    return dict(tm_in=512, tq=min(1024, seq), tk=min(512, seq), tc=min(256, seq // SUBLANES), n_grp=4,
                tm_ffn=512)


def kernel(x_prompt, x_sample, norm_mix, w_in, conv_w, conv_b, w_rg, b_rg, w_ig, b_ig, lru_lambda, lambda_q1, lambda_k1, lambda_q2, lambda_k2, subln_g, w_out, norm_ffn, w_gate, w_up, w_down, norm_final):
    p = dict(norm_mix=norm_mix, w_in=w_in, conv_w=conv_w, conv_b=conv_b, w_rg=w_rg, b_rg=b_rg,
             w_ig=w_ig, b_ig=b_ig, lru_lambda=lru_lambda, lambda_q1=lambda_q1, lambda_k1=lambda_k1,
             lambda_q2=lambda_q2, lambda_k2=lambda_k2, subln_g=subln_g, w_out=w_out,
             norm_ffn=norm_ffn, w_gate=w_gate, w_up=w_up, w_down=w_down, norm_final=norm_final)
    y_prompt = _trunk(x_prompt, p, **_tiles(x_prompt.shape[1]))
    y_sample = _trunk(x_sample, p, **_tiles(x_sample.shape[1]))
    return (y_prompt, y_sample)
```

```python
import functools
import math

import jax
import jax.numpy as jnp
from jax import lax
from jax.experimental import pallas as pl
from jax.experimental.pallas import tpu as pltpu

F32 = jnp.float32
BF16 = jnp.bfloat16

N_HEADS = 4
HEAD_DV = 128
HEAD_DK = 64
N_LRU_BLOCKS = 8
CONV_W = 4
RG_C = 8.0
NORM_EPS = 1e-6
LOG2E = math.log2(math.e)
LANES = 128
SUBLANES = 8
VMEM_LIMIT = 56 * 1024 * 1024


def _rmsnorm(x, g):
    ms = jnp.mean(x * x, axis=-1, keepdims=True)
    return (x * lax.rsqrt(ms + NORM_EPS)) * g


def _inproj_kernel(x_ref, g_ref, w_ref, q_ref, k_ref, v_ref, xr_ref, gate_ref, *, qk_w, att_w, lru_w):
    h = _rmsnorm(x_ref[...], g_ref[...]).astype(BF16)

    def proj(lo, width):
        return jnp.dot(h, w_ref[:, lo:lo + width], preferred_element_type=F32)

    q_ref[...] = (proj(0, qk_w) * (LOG2E / math.sqrt(HEAD_DK))).astype(BF16)
    k_ref[...] = proj(qk_w, qk_w).astype(BF16)
    v_ref[...] = proj(2 * qk_w, att_w).astype(BF16)
    xr_ref[...] = proj(2 * qk_w + att_w, lru_w)
    gate_ref[...] = proj(2 * qk_w + att_w + lru_w, lru_w)


def _inproj(x2d, norm_g, w_in_bf16, *, tm):
    n_tok, d_model = x2d.shape
    qk_w = N_HEADS * 2 * HEAD_DK
    att_w = N_HEADS * HEAD_DV
    lru_w = (w_in_bf16.shape[1] - 2 * qk_w - att_w) // 2
    grid = (n_tok // tm,)
    row = lambda i: (i, 0)
    const = lambda i: (0, 0)
    return pl.pallas_call(
        functools.partial(_inproj_kernel, qk_w=qk_w, att_w=att_w, lru_w=lru_w),
        grid=grid,
        in_specs=[
            pl.BlockSpec((tm, d_model), row),
            pl.BlockSpec((1, d_model), const),
            pl.BlockSpec(w_in_bf16.shape, const, pipeline_mode=pl.Buffered(1)),
        ],
        out_specs=[
            pl.BlockSpec((tm, qk_w), row),
            pl.BlockSpec((tm, qk_w), row),
            pl.BlockSpec((tm, att_w), row),
            pl.BlockSpec((tm, lru_w), row),
            pl.BlockSpec((tm, lru_w), row),
        ],
        out_shape=[
            jax.ShapeDtypeStruct((n_tok, qk_w), BF16),
            jax.ShapeDtypeStruct((n_tok, qk_w), BF16),
            jax.ShapeDtypeStruct((n_tok, att_w), BF16),
            jax.ShapeDtypeStruct((n_tok, lru_w), F32),
            jax.ShapeDtypeStruct((n_tok, lru_w), F32),
        ],
        compiler_params=pltpu.CompilerParams(
            dimension_semantics=("arbitrary",), vmem_limit_bytes=VMEM_LIMIT),
        name="inproj",
    )(x2d, norm_g.reshape(1, d_model), w_in_bf16)


def _attn_kernel(slopes_ref, q_ref, k_ref, v_ref, lq1_ref, lk1_ref, lq2_ref, lk2_ref, g_ref,
                 o_ref, bias_ref, q2_ref, kfeat_ref, s_ref, p_ref, alpha_ref, m_ref, l_ref, acc_ref,
                 *, seq, tq, tk, lam_init):
    h = pl.program_id(0)
    b = pl.program_id(1)
    qi = pl.program_id(2)
    n_kc = seq // tk
    n_diag = tq // tk
    assert n_diag == 2 and n_kc >= n_diag and n_kc % 2 == 0
    slope2 = LOG2E * slopes_ref[h]

    def split3(x):
        hi = x.astype(BF16).astype(F32)
        mid = (x - hi).astype(BF16).astype(F32)
        return hi, mid, x - hi - mid

    def pieces(x, first):
        lane = lax.broadcasted_iota(jnp.int32, x.shape, x.ndim - 1)
        hi, mid, lo = split3(x)
        return jnp.where(lane == first, hi, jnp.where(lane == first + 1, mid,
                                                      jnp.where(lane == first + 2, lo, 0.0)))

    @pl.when((b == 0) & (qi == 0))
    def _():
        r = lax.broadcasted_iota(jnp.int32, (tq, tq), 0)
        u = lax.broadcasted_iota(jnp.int32, (tq, tq), 1)
        bias_ref[...] = -slope2 * jnp.abs(r - u).astype(F32)
        lane = lax.broadcasted_iota(jnp.int32, (tq, LANES), 1)
        rows = lax.broadcasted_iota(jnp.int32, (tq, LANES), 0).astype(F32)
        ones_q = jnp.where((lane >= 3) & (lane < 9), 1.0, 0.0)
        lane_k = lax.broadcasted_iota(jnp.int32, (tk, LANES), 1)
        cols = lax.broadcasted_iota(jnp.int32, (tk, LANES), 0).astype(F32)
        ones_k = jnp.where(lane_k < 3, 1.0, 0.0)
        for side, sg in enumerate((1.0, -1.0)):
            qf = (pieces(-slope2 * sg * rows, 0) + ones_q).astype(BF16)
            q2_ref[side, 0:tq, LANES:] = qf
            q2_ref[side, tq:2 * tq, LANES:] = qf
            kfeat_ref[side] = (pieces(slope2 * sg * cols, 3) + ones_k).astype(BF16)

    q = q_ref[0]
    lane = lax.broadcasted_iota(jnp.int32, q.shape, 1)
    zero = jnp.zeros_like(q)
    for side in range(2):
        q2_ref[side, 0:tq, :LANES] = jnp.where(lane < HEAD_DK, q, zero)
        q2_ref[side, tq:2 * tq, :LANES] = jnp.where(lane >= HEAD_DK, q, zero)

    m_ref[...] = jnp.full(m_ref.shape, -jnp.inf, F32)
    l_ref[...] = jnp.zeros(l_ref.shape, F32)
    acc_ref[...] = jnp.zeros(acc_ref.shape, F32)

    def chunk_of(i):
        o = i - n_diag
        return jnp.where(i < n_diag, n_diag * qi + i, o + jnp.where(o >= n_diag * qi, n_diag, 0))

    def scores_diag(j, slot):
        ks = pl.multiple_of(chunk_of(j) * tk, tk)
        s = lax.dot_general(q2_ref[0, :, :LANES], k_ref[0, pl.ds(ks, tk), :], (((1,), (1,)), ((), ())),
                            preferred_element_type=F32)
        s_ref[slot] = s.reshape(2, tq, tk) + bias_ref[:, j * tk:(j + 1) * tk][None]

    def scores(i, slot):
        c = chunk_of(i)
        ks = pl.multiple_of(c * tk, tk)
        side = (c >= n_diag * qi).astype(jnp.int32)
        sg = 1.0 - 2.0 * side.astype(F32)
        delta = (qi * tq - c * tk).astype(F32)
        dvec = jnp.full((1, LANES), -slope2 * sg * delta, F32)
        kf = kfeat_ref[side] + pieces(dvec, 6).astype(BF16)
        rhs = jnp.concatenate([k_ref[0, pl.ds(ks, tk), :], kf], axis=1)
        s = lax.dot_general(q2_ref[side], rhs, (((1,), (1,)), ((), ())), preferred_element_type=F32)
        s_ref[slot] = s.reshape(2, tq, tk)

    def softmax(slot):
        tiles = [s_ref[slot, :, :, j * LANES:(j + 1) * LANES] for j in range(tk // LANES)]
        m_old = m_ref[...]
        m_new = jnp.maximum(m_old, jnp.max(functools.reduce(jnp.maximum, tiles), axis=-1, keepdims=True))
        alpha = jnp.exp2(m_old - m_new)
        l_new = alpha * l_ref[...]
        for j, s in enumerate(tiles):
            p = jnp.exp2(s - m_new)
            l_new = l_new + p
            p_ref[slot, :, :, j * LANES:(j + 1) * LANES] = p.astype(BF16)
        l_ref[...] = l_new
        m_ref[...] = m_new
        alpha_ref[slot] = alpha

    def values(i, slot):
        ks = pl.multiple_of(chunk_of(i) * tk, tk)
        pv = jnp.dot(p_ref[slot].reshape(2 * tq, tk), v_ref[0, pl.ds(ks, tk), :],
                     preferred_element_type=F32)
        acc_ref[...] = alpha_ref[slot] * acc_ref[...] + pv.reshape(2, tq, HEAD_DV)

    def step(i, slot):
        values(i - 1, 1 - slot)
        softmax(slot)
        scores(i + 1, 1 - slot)

    scores_diag(0, 0)
    scores_diag(1, 1)
    softmax(0)

    if n_kc > 2:
        def two_steps(n, _):
            step(2 * n + 1, 1)
            step(2 * n + 2, 0)
            return 0

        lax.fori_loop(0, (n_kc - 2) // 2, two_steps, 0)
    values(n_kc - 2, n_kc % 2)
    softmax((n_kc - 1) % 2)
    values(n_kc - 1, (n_kc - 1) % 2)

    lam = (jnp.exp(jnp.sum(lq1_ref[...] * lk1_ref[...], axis=-1, keepdims=True))
           - jnp.exp(jnp.sum(lq2_ref[...] * lk2_ref[...], axis=-1, keepdims=True)) + lam_init)
    l = jnp.sum(l_ref[...], axis=-1, keepdims=True)
    o = acc_ref[0] / l[0] - lam * (acc_ref[1] / l[1])
    o_ref[0] = (_rmsnorm(o, g_ref[...]) * (1.0 - lam_init)).astype(BF16)


def _attention(q, k, v, lq1, lk1, lq2, lk2, subln_g, *, lam_init, tq, tk):
    batch, seq, _ = q.shape
    slopes = jnp.asarray([2.0 ** (-8.0 * (h + 1) / N_HEADS) for h in range(N_HEADS)], F32)
    grid = (N_HEADS, batch, seq // tq)
    vec = lambda a: a.reshape(1, -1).astype(F32)
    small = lambda n: pl.BlockSpec((1, n), lambda h, b, i: (0, 0))
    return pl.pallas_call(
        functools.partial(_attn_kernel, seq=seq, tq=tq, tk=tk, lam_init=lam_init),
        grid=grid,
        in_specs=[
            pl.BlockSpec(memory_space=pltpu.SMEM),
            pl.BlockSpec((1, tq, HEAD_DV), lambda h, b, i: (b, i, h)),
            pl.BlockSpec((1, seq, HEAD_DV), lambda h, b, i: (b, 0, h)),
            pl.BlockSpec((1, seq, HEAD_DV), lambda h, b, i: (b, 0, h)),
            small(HEAD_DK), small(HEAD_DK), small(HEAD_DK), small(HEAD_DK), small(HEAD_DV),
        ],
        out_specs=pl.BlockSpec((1, tq, HEAD_DV), lambda h, b, i: (b, i, h)),
        out_shape=jax.ShapeDtypeStruct((batch, seq, N_HEADS * HEAD_DV), BF16),
        scratch_shapes=[
            pltpu.VMEM((tq, tq), F32),
            pltpu.VMEM((2, 2 * tq, 2 * HEAD_DV), BF16),
            pltpu.VMEM((2, tk, LANES), BF16),
            pltpu.VMEM((2, 2, tq, tk), F32),
            pltpu.VMEM((2, 2, tq, tk), BF16),
            pltpu.VMEM((2, 2, tq, LANES), F32),
            pltpu.VMEM((2, tq, LANES), F32),
            pltpu.VMEM((2, tq, LANES), F32),
            pltpu.VMEM((2, tq, HEAD_DV), F32),
        ],
        compiler_params=pltpu.CompilerParams(
            dimension_semantics=("arbitrary", "arbitrary", "arbitrary"),
            vmem_limit_bytes=VMEM_LIMIT),
        name="diffattn",
    )(slopes, q, k, v, vec(lq1), vec(lk1), vec(lq2), vec(lk2), vec(subln_g))


def _sigmoid(x):
    return 1.0 / (1.0 + jnp.exp(-x))


def _softplus(x):
    return jnp.maximum(x, 0.0) + jnp.log1p(jnp.exp(-jnp.abs(x)))


def _lru_kernel(xr_ref, gate_ref, cw_ref, cb_ref, wg_ref, bg_ref, lam_ref, y_ref,
                xpad_ref, a_ref, b_ref, h_ref, *, seq, tc, n_grp):
    pad = SUBLANES

    xpad_ref[0:pad, :] = jnp.zeros((pad, LANES), F32)
    xpad_ref[pad + seq:pad + seq + pad, :] = jnp.zeros((pad, LANES), F32)
    xpad_ref[pad:pad + seq, :] = xr_ref[0]

    cw = cw_ref[...]
    cb = cb_ref[...]
    kk = _softplus(-lam_ref[0]) * (-0.5 * RG_C * LOG2E)

    def gates(c, _):
        t0 = pl.multiple_of(c * tc, tc)
        xc = cb
        for j in range(CONV_W):
            xc = xc + xpad_ref[pl.ds(t0 + pad - 2 + j, tc), :] * cw[j:j + 1, :]
        th = jnp.tanh(jnp.dot(xc.astype(BF16), wg_ref[0], preferred_element_type=F32) + bg_ref[0])
        xh = 0.5 * xc
        for d in range(2):
            t_r = th[:, (2 * d) * LANES:(2 * d + 1) * LANES]
            t_i = th[:, (2 * d + 1) * LANES:(2 * d + 2) * LANES]
            kd = kk[d:d + 1, :]
            a = jnp.exp2(kd * t_r + kd)
            m2 = jnp.maximum(1.0 - a * a, 1e-12)
            b = (m2 * lax.rsqrt(m2)) * (t_i * xh + xh)
            a_ref[d, pl.ds(t0, tc), :] = a
            b_ref[d, pl.ds(t0, tc), :] = b
        return 0

    part = seq // SUBLANES + 4
    assert (part // 4) % 2 == 1 and part % n_grp == 0
    seg = part // n_grp
    n_pad = SUBLANES * part - seq
    for d in range(2):
        a_ref[d, seq:seq + n_pad, :] = jnp.ones((n_pad, LANES), F32)
        b_ref[d, seq:seq + n_pad, :] = jnp.zeros((n_pad, LANES), F32)

    def tile(g, t):
        return pl.ds(g * seg + t, SUBLANES, stride=part)

    lax.fori_loop(0, seq // tc, gates, 0, unroll=2)

    zeros = jnp.zeros((SUBLANES, LANES), F32)
    ones = jnp.ones((SUBLANES, LANES), F32)

    def pass1(n, carry):
        out = []
        for d in range(2):
            t = n if d == 0 else seg - 1 - n
            hs, ps = carry[d]
            nh, npr = [], []
            for g in range(n_grp):
                a = a_ref[d, tile(g, t), :]
                nh.append(a * hs[g] + b_ref[d, tile(g, t), :])
                npr.append(a * ps[g])
            out.append((tuple(nh), tuple(npr)))
        return tuple(out)

    ends = lax.fori_loop(0, seg, pass1,
                         tuple(((zeros,) * n_grp, (ones,) * n_grp) for _ in range(2)), unroll=2)

    def chain_rows(e, q, reverse):
        row = lax.broadcasted_iota(jnp.int32, (SUBLANES, LANES), 0)
        edge = (SUBLANES - 1) if reverse else 0
        shift = (SUBLANES - 1) if reverse else 1
        carry = zeros
        for _ in range(SUBLANES - 1):
            carry = jnp.where(row == edge, 0.0, pltpu.roll(e + q * carry, shift, 0))
        return carry

    def entry_states(h_end, p_end, reverse):
        order = range(n_grp - 1, -1, -1) if reverse else range(n_grp)
        local, prod = {}, {}
        c, q = zeros, ones
        for g in order:
            local[g], prod[g] = c, q
            c = h_end[g] + p_end[g] * c
            q = p_end[g] * q
        row_in = chain_rows(c, q, reverse)
        return tuple(local[g] + prod[g] * row_in for g in range(n_grp))

    def pass2(n, carry):
        out = []
        for d in range(2):
            t = n if d == 0 else seg - 1 - n
            nh = []
            for g in range(n_grp):
                hcur = a_ref[d, tile(g, t), :] * carry[d][g] + b_ref[d, tile(g, t), :]
                h_ref[d, tile(g, t), :] = hcur
                nh.append(hcur)
            out.append(tuple(nh))
        return tuple(out)

    lax.fori_loop(0, seg, pass2,
                  tuple(entry_states(ends[d][0], ends[d][1], d == 1) for d in range(2)), unroll=2)

    c1 = math.sqrt(2.0 / math.pi)

    def finish(c, _):
        t0 = pl.multiple_of(c * tc, tc)
        hsum = h_ref[0, pl.ds(t0, tc), :] + h_ref[1, pl.ds(t0, tc), :]
        x = gate_ref[0, pl.ds(t0, tc), :]
        th = jnp.tanh(x * ((x * x) * (c1 * 0.044715) + c1))
        xh = 0.5 * x
        y_ref[0, pl.ds(t0, tc), :] = (hsum * (xh * th + xh)).astype(BF16)
        return 0

    lax.fori_loop(0, seq // tc, finish, 0)


def _lru_gate_weights(w_rg, b_rg, w_ig, b_ig):
    blk = w_rg.shape[-1]
    per = LANES // blk
    n_groups = N_LRU_BLOCKS // per

    def dense(w):
        w = w.reshape(n_groups, per, blk, blk)
        eye = jnp.eye(per, dtype=w.dtype)
        return jnp.einsum('gpcd,pq->gpcqd', w, eye).reshape(n_groups, LANES, LANES)

    wg = jnp.concatenate([dense(w_rg[0]), dense(w_ig[0]), dense(w_rg[1]), dense(w_ig[1])], axis=-1)
    bias = lambda v: v.reshape(n_groups, 1, LANES)
    bg = jnp.concatenate([bias(b_rg[0]), bias(b_ig[0]), bias(b_rg[1]), bias(b_ig[1])], axis=-1)
    return (0.5 * wg).astype(BF16), (0.5 * bg).astype(F32)


def _rglru(xr, gate, conv_w, conv_b, wg, bg, lru_lambda, *, tc, n_grp):
    batch, seq, lru_w = xr.shape
    n_groups = lru_w // LANES
    lam = lru_lambda.reshape(2, n_groups, LANES).transpose(1, 0, 2)
    grid = (batch, n_groups)
    tok = pl.BlockSpec((1, seq, LANES), lambda b, j: (b, 0, j))
    return pl.pallas_call(
        functools.partial(_lru_kernel, seq=seq, tc=tc, n_grp=n_grp),
        grid=grid,
        in_specs=[
            tok, tok,
            pl.BlockSpec((CONV_W, LANES), lambda b, j: (0, j)),
            pl.BlockSpec((1, LANES), lambda b, j: (0, j)),
            pl.BlockSpec((1, LANES, 4 * LANES), lambda b, j: (j, 0, 0)),
            pl.BlockSpec((1, 1, 4 * LANES), lambda b, j: (j, 0, 0)),
            pl.BlockSpec((1, 2, LANES), lambda b, j: (j, 0, 0)),
        ],
        out_specs=tok,
        out_shape=jax.ShapeDtypeStruct((batch, seq, lru_w), BF16),
        scratch_shapes=[
            pltpu.VMEM((seq + 2 * SUBLANES, LANES), F32),
            pltpu.VMEM((2, seq + 4 * SUBLANES, LANES), F32),
            pltpu.VMEM((2, seq + 4 * SUBLANES, LANES), F32),
            pltpu.VMEM((2, seq + 4 * SUBLANES, LANES), F32),
        ],
        compiler_params=pltpu.CompilerParams(
            dimension_semantics=("arbitrary", "arbitrary"), vmem_limit_bytes=VMEM_LIMIT),
        name="rglru",
    )(xr, gate, conv_w, conv_b.reshape(1, lru_w), wg, bg, lam)


def _ffn_kernel(x_ref, o_ref, y_ref, wo_ref, g2_ref, wg_ref, wu_ref, wd_ref, g3_ref, out_ref, *, att_w):
    x1 = (x_ref[...]
          + jnp.dot(o_ref[...], wo_ref[0:att_w, :], preferred_element_type=F32)
          + jnp.dot(y_ref[...], wo_ref[att_w:, :], preferred_element_type=F32))
    h2 = _rmsnorm(x1, g2_ref[...]).astype(BF16)
    gt = jnp.dot(h2, wg_ref[...], preferred_element_type=F32)
    up = jnp.dot(h2, wu_ref[...], preferred_element_type=F32)
    act = ((gt * _sigmoid(gt)) * up).astype(BF16)
    x2 = x1 + jnp.dot(act, wd_ref[...], preferred_element_type=F32)
    out_ref[...] = _rmsnorm(x2, g3_ref[...])


def _out_ffn(x2d, o2d, y2d, w_out, norm_ffn, w_gate, w_up, w_down, norm_final, *, tm):
    n_tok, d_model = x2d.shape
    att_w = o2d.shape[1]
    lru_w = y2d.shape[1]
    row = lambda i: (i, 0)
    const = lambda i: (0, 0)
    resident = lambda a: pl.BlockSpec(a.shape, const, pipeline_mode=pl.Buffered(1))
    g2 = norm_ffn.reshape(1, d_model)
    g3 = norm_final.reshape(1, d_model)
    return pl.pallas_call(
        functools.partial(_ffn_kernel, att_w=att_w),
        grid=(n_tok // tm,),
        in_specs=[
            pl.BlockSpec((tm, d_model), row),
            pl.BlockSpec((tm, att_w), row),
            pl.BlockSpec((tm, lru_w), row),
            resident(w_out), resident(g2), resident(w_gate), resident(w_up), resident(w_down),
            resident(g3),
        ],
        out_specs=pl.BlockSpec((tm, d_model), row),
        out_shape=jax.ShapeDtypeStruct((n_tok, d_model), F32),
        compiler_params=pltpu.CompilerParams(
            dimension_semantics=("arbitrary",), vmem_limit_bytes=VMEM_LIMIT),
        name="out_ffn",
    )(x2d, o2d, y2d, w_out, g2, w_gate, w_up, w_down, g3)


def _trunk(x, p, *, tm_in, tq, tk, tc, n_grp, tm_ffn):
    batch, seq, d_model = x.shape
    x2d = x.reshape(batch * seq, d_model)
    depth = p["w_in"].shape[0]
    for l in range(depth):
        last = l == depth - 1
        lam_init = 0.8 - 0.6 * math.exp(-0.3 * l)
        q, k, v, xr, gate = _inproj(x2d, p["norm_mix"][l], p["w_in"][l].astype(BF16), tm=tm_in)
        att_w = v.shape[1]
        lru_w = xr.shape[1]
        shp = lambda a: a.reshape(batch, seq, a.shape[1])
        o = _attention(shp(q), shp(k), shp(v), p["lambda_q1"][l], p["lambda_k1"][l],
                       p["lambda_q2"][l], p["lambda_k2"][l], p["subln_g"][l],
                       lam_init=lam_init, tq=tq, tk=tk)
        wg, bg = _lru_gate_weights(p["w_rg"][l], p["b_rg"][l], p["w_ig"][l], p["b_ig"][l])
        y = _rglru(shp(xr), shp(gate), p["conv_w"][l], p["conv_b"][l], wg, bg, p["lru_lambda"][l],
                   tc=tc, n_grp=n_grp)
        assert last, "only the last layer's fused final norm is implemented"
        x2d = _out_ffn(x2d, o.reshape(batch * seq, att_w), y.reshape(batch * seq, lru_w),
                       p["w_out"][l].astype(BF16), p["norm_ffn"][l], p["w_gate"][l].astype(BF16),
                       p["w_up"][l].astype(BF16), p["w_down"][l].astype(BF16), p["norm_final"],
                       tm=tm_ffn)
    return x2d.reshape(batch, seq, d_model)


def _tiles(seq):
    return dict(tm_in=512, tq=min(1024, seq), tk=min(512, seq), tc=min(256, seq // SUBLANES), n_grp=4,
                tm_ffn=512)


def kernel(x_prompt, x_sample, norm_mix, w_in, conv_w, conv_b, w_rg, b_rg, w_ig, b_ig, lru_lambda, lambda_q1, lambda_k1, lambda_q2, lambda_k2, subln_g, w_out, norm_ffn, w_gate, w_up, w_down, norm_final):
    p = dict(norm_mix=norm_mix, w_in=w_in, conv_w=conv_w, conv_b=conv_b, w_rg=w_rg, b_rg=b_rg,
             w_ig=w_ig, b_ig=b_ig, lru_lambda=lru_lambda, lambda_q1=lambda_q1, lambda_k1=lambda_k1,
             lambda_q2=lambda_q2, lambda_k2=lambda_k2, subln_g=subln_g, w_out=w_out,
             norm_ffn=norm_ffn, w_gate=w_gate, w_up=w_up, w_down=w_down, norm_final=norm_final)
    y_prompt = _trunk(x_prompt, p, **_tiles(x_prompt.shape[1]))
    y_sample = _trunk(x_sample, p, **_tiles(x_sample.shape[1]))
    return (y_prompt, y_sample)
```

```python
import functools
import math

import jax
import jax.numpy as jnp
from jax import lax
from jax.experimental import pallas as pl
from jax.experimental.pallas import tpu as pltpu

F32 = jnp.float32
BF16 = jnp.bfloat16

N_HEADS = 4
HEAD_DV = 128
HEAD_DK = 64
N_LRU_BLOCKS = 8
CONV_W = 4
RG_C = 8.0
NORM_EPS = 1e-6
LOG2E = math.log2(math.e)
LANES = 128
SUBLANES = 8
VMEM_LIMIT = 56 * 1024 * 1024


def _rmsnorm(x, g):
    ms = jnp.mean(x * x, axis=-1, keepdims=True)
    return (x * lax.rsqrt(ms + NORM_EPS)) * g


def _inproj_kernel(x_ref, g_ref, w_ref, q_ref, k_ref, v_ref, xr_ref, gate_ref, *, qk_w, att_w, lru_w):
    h = _rmsnorm(x_ref[...], g_ref[...]).astype(BF16)

    def proj(lo, width):
        return jnp.dot(h, w_ref[:, lo:lo + width], preferred_element_type=F32)

    q_ref[...] = (proj(0, qk_w) * (LOG2E / math.sqrt(HEAD_DK))).astype(BF16)
    k_ref[...] = proj(qk_w, qk_w).astype(BF16)
    v_ref[...] = proj(2 * qk_w, att_w).astype(BF16)
    xr_ref[...] = proj(2 * qk_w + att_w, lru_w)
    gate_ref[...] = proj(2 * qk_w + att_w + lru_w, lru_w)


def _inproj(x2d, norm_g, w_in_bf16, *, tm):
    n_tok, d_model = x2d.shape
    qk_w = N_HEADS * 2 * HEAD_DK
    att_w = N_HEADS * HEAD_DV
    lru_w = (w_in_bf16.shape[1] - 2 * qk_w - att_w) // 2
    grid = (n_tok // tm,)
    row = lambda i: (i, 0)
    const = lambda i: (0, 0)
    return pl.pallas_call(
        functools.partial(_inproj_kernel, qk_w=qk_w, att_w=att_w, lru_w=lru_w),
        grid=grid,
        in_specs=[
            pl.BlockSpec((tm, d_model), row),
            pl.BlockSpec((1, d_model), const),
            pl.BlockSpec(w_in_bf16.shape, const, pipeline_mode=pl.Buffered(1)),
        ],
        out_specs=[
            pl.BlockSpec((tm, qk_w), row),
            pl.BlockSpec((tm, qk_w), row),
            pl.BlockSpec((tm, att_w), row),
            pl.BlockSpec((tm, lru_w), row),
            pl.BlockSpec((tm, lru_w), row),
        ],
        out_shape=[
            jax.ShapeDtypeStruct((n_tok, qk_w), BF16),
            jax.ShapeDtypeStruct((n_tok, qk_w), BF16),
            jax.ShapeDtypeStruct((n_tok, att_w), BF16),
            jax.ShapeDtypeStruct((n_tok, lru_w), F32),
            jax.ShapeDtypeStruct((n_tok, lru_w), F32),
        ],
        compiler_params=pltpu.CompilerParams(
            dimension_semantics=("arbitrary",), vmem_limit_bytes=VMEM_LIMIT),
        name="inproj",
    )(x2d, norm_g.reshape(1, d_model), w_in_bf16)


def _attn_kernel(slopes_ref, q_ref, k_ref, v_ref, lq1_ref, lk1_ref, lq2_ref, lk2_ref, g_ref,
                 o_ref, bias_ref, q2_ref, kfeat_ref, s_ref, p_ref, alpha_ref, m_ref, l_ref, acc_ref,
                 *, seq, tq, tk, lam_init):
    h = pl.program_id(0)
    b = pl.program_id(1)
    qi = pl.program_id(2)
    n_kc = seq // tk
    n_diag = tq // tk
    assert n_diag == 2 and n_kc >= n_diag and n_kc % 2 == 0
    slope2 = LOG2E * slopes_ref[h]

    def split3(x):
        hi = x.astype(BF16).astype(F32)
        mid = (x - hi).astype(BF16).astype(F32)
        return hi, mid, x - hi - mid

    def pieces(x, first):
        lane = lax.broadcasted_iota(jnp.int32, x.shape, x.ndim - 1)
        hi, mid, lo = split3(x)
        return jnp.where(lane == first, hi, jnp.where(lane == first + 1, mid,
                                                      jnp.where(lane == first + 2, lo, 0.0)))

    @pl.when((b == 0) & (qi == 0))
    def _():
        r = lax.broadcasted_iota(jnp.int32, (tq, tq), 0)
        u = lax.broadcasted_iota(jnp.int32, (tq, tq), 1)
        bias_ref[...] = -slope2 * jnp.abs(r - u).astype(F32)
        lane = lax.broadcasted_iota(jnp.int32, (tq, LANES), 1)
        rows = lax.broadcasted_iota(jnp.int32, (tq, LANES), 0).astype(F32)
        ones_q = jnp.where((lane >= 3) & (lane < 9), 1.0, 0.0)
        lane_k = lax.broadcasted_iota(jnp.int32, (tk, LANES), 1)
        cols = lax.broadcasted_iota(jnp.int32, (tk, LANES), 0).astype(F32)
        ones_k = jnp.where(lane_k < 3, 1.0, 0.0)
        for side, sg in enumerate((1.0, -1.0)):
            qf = (pieces(-slope2 * sg * rows, 0) + ones_q).astype(BF16)
            q2_ref[side, 0:tq, LANES:] = qf
            q2_ref[side, tq:2 * tq, LANES:] = qf
            kfeat_ref[side] = (pieces(slope2 * sg * cols, 3) + ones_k).astype(BF16)

    q = q_ref[0]
    lane = lax.broadcasted_iota(jnp.int32, q.shape, 1)
    zero = jnp.zeros_like(q)
    for side in range(2):
        q2_ref[side, 0:tq, :LANES] = jnp.where(lane < HEAD_DK, q, zero)
        q2_ref[side, tq:2 * tq, :LANES] = jnp.where(lane >= HEAD_DK, q, zero)

    m_ref[...] = jnp.full(m_ref.shape, -jnp.inf, F32)
    l_ref[...] = jnp.zeros(l_ref.shape, F32)
    acc_ref[...] = jnp.zeros(acc_ref.shape, F32)

    def chunk_of(i):
        o = i - n_diag
        return jnp.where(i < n_diag, n_diag * qi + i, o + jnp.where(o >= n_diag * qi, n_diag, 0))

    def scores_diag(j, slot):
        ks = pl.multiple_of(chunk_of(j) * tk, tk)
        s = lax.dot_general(q2_ref[0, :, :LANES], k_ref[0, pl.ds(ks, tk), :], (((1,), (1,)), ((), ())),
                            preferred_element_type=F32)
        s_ref[slot, :, :, :tk] = s.reshape(2, tq, tk) + bias_ref[:, j * tk:(j + 1) * tk][None]

    def scores(i, slot):
        c = chunk_of(i)
        ks = pl.multiple_of(c * tk, tk)
        side = (c >= n_diag * qi).astype(jnp.int32)
        sg = 1.0 - 2.0 * side.astype(F32)
        delta = (qi * tq - c * tk).astype(F32)
        dvec = jnp.full((1, LANES), -slope2 * sg * delta, F32)
        kf = kfeat_ref[side] + pieces(dvec, 6).astype(BF16)
        rhs = jnp.concatenate([k_ref[0, pl.ds(ks, tk), :], kf], axis=1)
        s = lax.dot_general(q2_ref[side], rhs, (((1,), (1,)), ((), ())), preferred_element_type=F32)
        s_ref[slot, :, :, :tk] = s.reshape(2, tq, tk)

    def softmax(slot):
        tiles = [s_ref[slot, :, :, j * LANES:(j + 1) * LANES] for j in range(tk // LANES)]
        m_old = m_ref[...]
        m_new = jnp.maximum(m_old, jnp.max(functools.reduce(jnp.maximum, tiles), axis=-1, keepdims=True))
        alpha = jnp.exp2(m_old - m_new)
        l_new = alpha * l_ref[...]
        for j, s in enumerate(tiles):
            p = jnp.exp2(s - m_new)
            l_new = l_new + p
            p_ref[slot, :, :, j * LANES:(j + 1) * LANES] = p.astype(BF16)
        l_ref[...] = l_new
        m_ref[...] = m_new
        alpha_ref[slot] = alpha

    def values(i, slot):
        ks = pl.multiple_of(chunk_of(i) * tk, tk)
        pv = jnp.dot(p_ref[slot, :, :, :tk].reshape(2 * tq, tk), v_ref[0, pl.ds(ks, tk), :],
                     preferred_element_type=F32)
        acc_ref[...] = alpha_ref[slot] * acc_ref[...] + pv.reshape(2, tq, HEAD_DV)

    def step(i, slot):
        values(i - 1, 1 - slot)
        softmax(slot)
        scores(i + 1, 1 - slot)

    scores_diag(0, 0)
    scores_diag(1, 1)
    softmax(0)

    if n_kc > 2:
        def two_steps(n, _):
            step(2 * n + 1, 1)
            step(2 * n + 2, 0)
            return 0

        lax.fori_loop(0, (n_kc - 2) // 2, two_steps, 0)
    values(n_kc - 2, n_kc % 2)
    softmax((n_kc - 1) % 2)
    values(n_kc - 1, (n_kc - 1) % 2)

    lam = (jnp.exp(jnp.sum(lq1_ref[...] * lk1_ref[...], axis=-1, keepdims=True))
           - jnp.exp(jnp.sum(lq2_ref[...] * lk2_ref[...], axis=-1, keepdims=True)) + lam_init)
    l = jnp.sum(l_ref[...], axis=-1, keepdims=True)
    o = acc_ref[0] / l[0] - lam * (acc_ref[1] / l[1])
    o_ref[0] = (_rmsnorm(o, g_ref[...]) * (1.0 - lam_init)).astype(BF16)


def _attention(q, k, v, lq1, lk1, lq2, lk2, subln_g, *, lam_init, tq, tk):
    batch, seq, _ = q.shape
    slopes = jnp.asarray([2.0 ** (-8.0 * (h + 1) / N_HEADS) for h in range(N_HEADS)], F32)
    grid = (N_HEADS, batch, seq // tq)
    vec = lambda a: a.reshape(1, -1).astype(F32)
    small = lambda n: pl.BlockSpec((1, n), lambda h, b, i: (0, 0))
    return pl.pallas_call(
        functools.partial(_attn_kernel, seq=seq, tq=tq, tk=tk, lam_init=lam_init),
        grid=grid,
        in_specs=[
            pl.BlockSpec(memory_space=pltpu.SMEM),
            pl.BlockSpec((1, tq, HEAD_DV), lambda h, b, i: (b, i, h)),
            pl.BlockSpec((1, seq, HEAD_DV), lambda h, b, i: (b, 0, h)),
            pl.BlockSpec((1, seq, HEAD_DV), lambda h, b, i: (b, 0, h)),
            small(HEAD_DK), small(HEAD_DK), small(HEAD_DK), small(HEAD_DK), small(HEAD_DV),
        ],
        out_specs=pl.BlockSpec((1, tq, HEAD_DV), lambda h, b, i: (b, i, h)),
        out_shape=jax.ShapeDtypeStruct((batch, seq, N_HEADS * HEAD_DV), BF16),
        scratch_shapes=[
            pltpu.VMEM((tq, tq), F32),
            pltpu.VMEM((2, 2 * tq, 2 * HEAD_DV), BF16),
            pltpu.VMEM((2, tk, LANES), BF16),
            pltpu.VMEM((2, 2, tq, tk + LANES), F32),
            pltpu.VMEM((2, 2, tq, tk + LANES), BF16),
            pltpu.VMEM((2, 2, tq, LANES), F32),
            pltpu.VMEM((2, tq, LANES), F32),
            pltpu.VMEM((2, tq, LANES), F32),
            pltpu.VMEM((2, tq, HEAD_DV), F32),
        ],
        compiler_params=pltpu.CompilerParams(
            dimension_semantics=("arbitrary", "arbitrary", "arbitrary"),
            vmem_limit_bytes=VMEM_LIMIT),
        name="diffattn",
    )(slopes, q, k, v, vec(lq1), vec(lk1), vec(lq2), vec(lk2), vec(subln_g))


def _sigmoid(x):
    return 1.0 / (1.0 + jnp.exp(-x))


def _softplus(x):
    return jnp.maximum(x, 0.0) + jnp.log1p(jnp.exp(-jnp.abs(x)))


def _lru_kernel(xr_ref, gate_ref, cw_ref, cb_ref, wg_ref, bg_ref, lam_ref, y_ref,
                xpad_ref, a_ref, b_ref, h_ref, *, seq, tc, n_grp):
    pad = SUBLANES

    xpad_ref[0:pad, :] = jnp.zeros((pad, LANES), F32)
    xpad_ref[pad + seq:pad + seq + pad, :] = jnp.zeros((pad, LANES), F32)
    xpad_ref[pad:pad + seq, :] = xr_ref[0]

    cw = cw_ref[...]
    cb = cb_ref[...]
    kk = _softplus(-lam_ref[0]) * (-0.5 * RG_C * LOG2E)

    def gates(c, _):
        t0 = pl.multiple_of(c * tc, tc)
        xc = cb
        for j in range(CONV_W):
            xc = xc + xpad_ref[pl.ds(t0 + pad - 2 + j, tc), :] * cw[j:j + 1, :]
        th = jnp.tanh(jnp.dot(xc.astype(BF16), wg_ref[0], preferred_element_type=F32) + bg_ref[0])
        xh = 0.5 * xc
        for d in range(2):
            t_r = th[:, (2 * d) * LANES:(2 * d + 1) * LANES]
            t_i = th[:, (2 * d + 1) * LANES:(2 * d + 2) * LANES]
            kd = kk[d:d + 1, :]
            a = jnp.exp2(kd * t_r + kd)
            m2 = jnp.maximum(1.0 - a * a, 1e-12)
            b = (m2 * lax.rsqrt(m2)) * (t_i * xh + xh)
            a_ref[d, pl.ds(t0, tc), :] = a
            b_ref[d, pl.ds(t0, tc), :] = b
        return 0

    part = seq // SUBLANES + 4
    assert (part // 4) % 2 == 1 and part % n_grp == 0
    seg = part // n_grp
    n_pad = SUBLANES * part - seq
    for d in range(2):
        a_ref[d, seq:seq + n_pad, :] = jnp.ones((n_pad, LANES), F32)
        b_ref[d, seq:seq + n_pad, :] = jnp.zeros((n_pad, LANES), F32)

    def tile(g, t):
        return pl.ds(g * seg + t, SUBLANES, stride=part)

    lax.fori_loop(0, seq // tc, gates, 0, unroll=2)

    zeros = jnp.zeros((SUBLANES, LANES), F32)
    ones = jnp.ones((SUBLANES, LANES), F32)

    def pass1(n, carry):
        out = []
        for d in range(2):
            t = n if d == 0 else seg - 1 - n
            hs, ps = carry[d]
            nh, npr = [], []
            for g in range(n_grp):
                a = a_ref[d, tile(g, t), :]
                nh.append(a * hs[g] + b_ref[d, tile(g, t), :])
                npr.append(a * ps[g])
            out.append((tuple(nh), tuple(npr)))
        return tuple(out)

    ends = lax.fori_loop(0, seg, pass1,
                         tuple(((zeros,) * n_grp, (ones,) * n_grp) for _ in range(2)), unroll=2)

    def chain_rows(e, q, reverse):
        row = lax.broadcasted_iota(jnp.int32, (SUBLANES, LANES), 0)
        edge = (SUBLANES - 1) if reverse else 0
        shift = (SUBLANES - 1) if reverse else 1
        carry = zeros
        for _ in range(SUBLANES - 1):
            carry = jnp.where(row == edge, 0.0, pltpu.roll(e + q * carry, shift, 0))
        return carry

    def entry_states(h_end, p_end, reverse):
        order = range(n_grp - 1, -1, -1) if reverse else range(n_grp)
        local, prod = {}, {}
        c, q = zeros, ones
        for g in order:
            local[g], prod[g] = c, q
            c = h_end[g] + p_end[g] * c
            q = p_end[g] * q
        row_in = chain_rows(c, q, reverse)
        return tuple(local[g] + prod[g] * row_in for g in range(n_grp))

    def pass2(n, carry):
        out = []
        for d in range(2):
            t = n if d == 0 else seg - 1 - n
            nh = []
            for g in range(n_grp):
                hcur = a_ref[d, tile(g, t), :] * carry[d][g] + b_ref[d, tile(g, t), :]
                h_ref[d, tile(g, t), :] = hcur
                nh.append(hcur)
            out.append(tuple(nh))
        return tuple(out)

    lax.fori_loop(0, seg, pass2,
                  tuple(entry_states(ends[d][0], ends[d][1], d == 1) for d in range(2)), unroll=2)

    c1 = math.sqrt(2.0 / math.pi)

    def finish(c, _):
        t0 = pl.multiple_of(c * tc, tc)
        hsum = h_ref[0, pl.ds(t0, tc), :] + h_ref[1, pl.ds(t0, tc), :]
        x = gate_ref[0, pl.ds(t0, tc), :]
        th = jnp.tanh(x * ((x * x) * (c1 * 0.044715) + c1))
        xh = 0.5 * x
        y_ref[0, pl.ds(t0, tc), :] = (hsum * (xh * th + xh)).astype(BF16)
        return 0

    lax.fori_loop(0, seq // tc, finish, 0)


def _lru_gate_weights(w_rg, b_rg, w_ig, b_ig):
    blk = w_rg.shape[-1]
    per = LANES // blk
    n_groups = N_LRU_BLOCKS // per

    def dense(w):
        w = w.reshape(n_groups, per, blk, blk)
        eye = jnp.eye(per, dtype=w.dtype)
        return jnp.einsum('gpcd,pq->gpcqd', w, eye).reshape(n_groups, LANES, LANES)

    wg = jnp.concatenate([dense(w_rg[0]), dense(w_ig[0]), dense(w_rg[1]), dense(w_ig[1])], axis=-1)
    bias = lambda v: v.reshape(n_groups, 1, LANES)
    bg = jnp.concatenate([bias(b_rg[0]), bias(b_ig[0]), bias(b_rg[1]), bias(b_ig[1])], axis=-1)
    return (0.5 * wg).astype(BF16), (0.5 * bg).astype(F32)


def _rglru(xr, gate, conv_w, conv_b, wg, bg, lru_lambda, *, tc, n_grp):
    batch, seq, lru_w = xr.shape
    n_groups = lru_w // LANES
    lam = lru_lambda.reshape(2, n_groups, LANES).transpose(1, 0, 2)
    grid = (batch, n_groups)
    tok = pl.BlockSpec((1, seq, LANES), lambda b, j: (b, 0, j))
    return pl.pallas_call(
        functools.partial(_lru_kernel, seq=seq, tc=tc, n_grp=n_grp),
        grid=grid,
        in_specs=[
            tok, tok,
            pl.BlockSpec((CONV_W, LANES), lambda b, j: (0, j)),
            pl.BlockSpec((1, LANES), lambda b, j: (0, j)),
            pl.BlockSpec((1, LANES, 4 * LANES), lambda b, j: (j, 0, 0)),
            pl.BlockSpec((1, 1, 4 * LANES), lambda b, j: (j, 0, 0)),
            pl.BlockSpec((1, 2, LANES), lambda b, j: (j, 0, 0)),
        ],
        out_specs=tok,
        out_shape=jax.ShapeDtypeStruct((batch, seq, lru_w), BF16),
        scratch_shapes=[
            pltpu.VMEM((seq + 2 * SUBLANES, LANES), F32),
            pltpu.VMEM((2, seq + 4 * SUBLANES, LANES), F32),
            pltpu.VMEM((2, seq + 4 * SUBLANES, LANES), F32),
            pltpu.VMEM((2, seq + 4 * SUBLANES, LANES), F32),
        ],
        compiler_params=pltpu.CompilerParams(
            dimension_semantics=("arbitrary", "arbitrary"), vmem_limit_bytes=VMEM_LIMIT),
        name="rglru",
    )(xr, gate, conv_w, conv_b.reshape(1, lru_w), wg, bg, lam)


def _ffn_kernel(x_ref, o_ref, y_ref, wo_ref, g2_ref, wg_ref, wu_ref, wd_ref, g3_ref, out_ref, *, att_w):
    x1 = (x_ref[...]
          + jnp.dot(o_ref[...], wo_ref[0:att_w, :], preferred_element_type=F32)
          + jnp.dot(y_ref[...], wo_ref[att_w:, :], preferred_element_type=F32))
    h2 = _rmsnorm(x1, g2_ref[...]).astype(BF16)
    gt = jnp.dot(h2, wg_ref[...], preferred_element_type=F32)
    up = jnp.dot(h2, wu_ref[...], preferred_element_type=F32)
    act = ((gt * _sigmoid(gt)) * up).astype(BF16)
    x2 = x1 + jnp.dot(act, wd_ref[...], preferred_element_type=F32)
    out_ref[...] = _rmsnorm(x2, g3_ref[...])


def _out_ffn(x2d, o2d, y2d, w_out, norm_ffn, w_gate, w_up, w_down, norm_final, *, tm):
    n_tok, d_model = x2d.shape
    att_w = o2d.shape[1]
    lru_w = y2d.shape[1]
    row = lambda i: (i, 0)
    const = lambda i: (0, 0)
    resident = lambda a: pl.BlockSpec(a.shape, const, pipeline_mode=pl.Buffered(1))
    g2 = norm_ffn.reshape(1, d_model)
    g3 = norm_final.reshape(1, d_model)
    return pl.pallas_call(
        functools.partial(_ffn_kernel, att_w=att_w),
        grid=(n_tok // tm,),
        in_specs=[
            pl.BlockSpec((tm, d_model), row),
            pl.BlockSpec((tm, att_w), row),
            pl.BlockSpec((tm, lru_w), row),
            resident(w_out), resident(g2), resident(w_gate), resident(w_up), resident(w_down),
            resident(g3),
        ],
        out_specs=pl.BlockSpec((tm, d_model), row),
        out_shape=jax.ShapeDtypeStruct((n_tok, d_model), F32),
        compiler_params=pltpu.CompilerParams(
            dimension_semantics=("arbitrary",), vmem_limit_bytes=VMEM_LIMIT),
        name="out_ffn",
    )(x2d, o2d, y2d, w_out, g2, w_gate, w_up, w_down, g3)


def _trunk(x, p, *, tm_in, tq, tk, tc, n_grp, tm_ffn):
    batch, seq, d_model = x.shape
    x2d = x.reshape(batch * seq, d_model)
    depth = p["w_in"].shape[0]
    for l in range(depth):
        last = l == depth - 1
        lam_init = 0.8 - 0.6 * math.exp(-0.3 * l)
        q, k, v, xr, gate = _inproj(x2d, p["norm_mix"][l], p["w_in"][l].astype(BF16), tm=tm_in)
        att_w = v.shape[1]
        lru_w = xr.shape[1]
        shp = lambda a: a.reshape(batch, seq, a.shape[1])
        o = _attention(shp(q), shp(k), shp(v), p["lambda_q1"][l], p["lambda_k1"][l],
                       p["lambda_q2"][l], p["lambda_k2"][l], p["subln_g"][l],
                       lam_init=lam_init, tq=tq, tk=tk)
        wg, bg = _lru_gate_weights(p["w_rg"][l], p["b_rg"][l], p["w_ig"][l], p["b_ig"][l])
        y = _rglru(shp(xr), shp(gate), p["conv_w"][l], p["conv_b"][l], wg, bg, p["lru_lambda"][l],
                   tc=tc, n_grp=n_grp)
        assert last, "only the last layer's fused final norm is implemented"
        x2d = _out_ffn(x2d, o.reshape(batch * seq, att_w), y.reshape(batch * seq, lru_w),
                       p["w_out"][l].astype(BF16), p["norm_ffn"][l], p["w_gate"][l].astype(BF16),
                       p["w_up"][l].astype(BF16), p["w_down"][l].astype(BF16), p["norm_final"],
                       tm=tm_ffn)
    return x2d.reshape(batch, seq, d_model)


def _tiles(seq):
    return dict(tm_in=512, tq=min(1024, seq), tk=min(512, seq), tc=min(256, seq // SUBLANES), n_grp=4,
                tm_ffn=512)


def kernel(x_prompt, x_sample, norm_mix, w_in, conv_w, conv_b, w_rg, b_rg, w_ig, b_ig, lru_lambda, lambda_q1, lambda_k1, lambda_q2, lambda_k2, subln_g, w_out, norm_ffn, w_gate, w_up, w_down, norm_final):
    p = dict(norm_mix=norm_mix, w_in=w_in, conv_w=conv_w, conv_b=conv_b, w_rg=w_rg, b_rg=b_rg,
             w_ig=w_ig, b_ig=b_ig, lru_lambda=lru_lambda, lambda_q1=lambda_q1, lambda_k1=lambda_k1,
             lambda_q2=lambda_q2, lambda_k2=lambda_k2, subln_g=subln_g, w_out=w_out,
             norm_ffn=norm_ffn, w_gate=w_gate, w_up=w_up, w_down=w_down, norm_final=norm_final)
    y_prompt = _trunk(x_prompt, p, **_tiles(x_prompt.shape[1]))
    y_sample = _trunk(x_sample, p, **_tiles(x_sample.shape[1]))
    return (y_prompt, y_sample)
```

```python
import functools
import math

import jax
import jax.numpy as jnp
from jax import lax
from jax.experimental import pallas as pl
from jax.experimental.pallas import tpu as pltpu

F32 = jnp.float32
BF16 = jnp.bfloat16

N_HEADS = 4
HEAD_DV = 128
HEAD_DK = 64
N_LRU_BLOCKS = 8
CONV_W = 4
RG_C = 8.0
NORM_EPS = 1e-6
LOG2E = math.log2(math.e)
LANES = 128
SUBLANES = 8
VMEM_LIMIT = 56 * 1024 * 1024


def _rmsnorm(x, g):
    ms = jnp.mean(x * x, axis=-1, keepdims=True)
    return (x * lax.rsqrt(ms + NORM_EPS)) * g


def _inproj_kernel(x_ref, g_ref, w_ref, q_ref, k_ref, v_ref, xr_ref, gate_ref, *, qk_w, att_w, lru_w):
    h = _rmsnorm(x_ref[...], g_ref[...]).astype(BF16)

    def proj(lo, width):
        return jnp.dot(h, w_ref[:, lo:lo + width], preferred_element_type=F32)

    q_ref[...] = (proj(0, qk_w) * (LOG2E / math.sqrt(HEAD_DK))).astype(BF16)
    k_ref[...] = proj(qk_w, qk_w).astype(BF16)
    v_ref[...] = proj(2 * qk_w, att_w).astype(BF16)
    xr_ref[...] = proj(2 * qk_w + att_w, lru_w)
    gate_ref[...] = proj(2 * qk_w + att_w + lru_w, lru_w)


def _inproj(x2d, norm_g, w_in_bf16, *, tm):
    n_tok, d_model = x2d.shape
    qk_w = N_HEADS * 2 * HEAD_DK
    att_w = N_HEADS * HEAD_DV
    lru_w = (w_in_bf16.shape[1] - 2 * qk_w - att_w) // 2
    grid = (n_tok // tm,)
    row = lambda i: (i, 0)
    const = lambda i: (0, 0)
    return pl.pallas_call(
        functools.partial(_inproj_kernel, qk_w=qk_w, att_w=att_w, lru_w=lru_w),
        grid=grid,
        in_specs=[
            pl.BlockSpec((tm, d_model), row),
            pl.BlockSpec((1, d_model), const),
            pl.BlockSpec(w_in_bf16.shape, const, pipeline_mode=pl.Buffered(1)),
        ],
        out_specs=[
            pl.BlockSpec((tm, qk_w), row),
            pl.BlockSpec((tm, qk_w), row),
            pl.BlockSpec((tm, att_w), row),
            pl.BlockSpec((tm, lru_w), row),
            pl.BlockSpec((tm, lru_w), row),
        ],
        out_shape=[
            jax.ShapeDtypeStruct((n_tok, qk_w), BF16),
            jax.ShapeDtypeStruct((n_tok, qk_w), BF16),
            jax.ShapeDtypeStruct((n_tok, att_w), BF16),
            jax.ShapeDtypeStruct((n_tok, lru_w), F32),
            jax.ShapeDtypeStruct((n_tok, lru_w), F32),
        ],
        compiler_params=pltpu.CompilerParams(
            dimension_semantics=("arbitrary",), vmem_limit_bytes=VMEM_LIMIT),
        name="inproj",
    )(x2d, norm_g.reshape(1, d_model), w_in_bf16)


def _attn_kernel(slopes_ref, q_ref, k_ref, v_ref, lq1_ref, lk1_ref, lq2_ref, lk2_ref, g_ref,
                 o_ref, bias_ref, q2_ref, kfeat_ref, s_ref, p_ref, alpha_ref, m_ref, l_ref, acc_ref,
                 *, seq, tq, tk, lam_init):
    h = pl.program_id(0)
    b = pl.program_id(1)
    n_kc = seq // tk
    n_diag = tq // tk
    assert n_diag == 2 and n_kc >= n_diag and n_kc % 2 == 0
    slope2 = LOG2E * slopes_ref[h]

    def split3(x):
        hi = x.astype(BF16).astype(F32)
        mid = (x - hi).astype(BF16).astype(F32)
        return hi, mid, x - hi - mid

    def pieces(x, first):
        lane = lax.broadcasted_iota(jnp.int32, x.shape, x.ndim - 1)
        hi, mid, lo = split3(x)
        return jnp.where(lane == first, hi, jnp.where(lane == first + 1, mid,
                                                      jnp.where(lane == first + 2, lo, 0.0)))

    @pl.when(b == 0)
    def _():
        r = lax.broadcasted_iota(jnp.int32, (tq, tq), 0)
        u = lax.broadcasted_iota(jnp.int32, (tq, tq), 1)
        bias_ref[...] = -slope2 * jnp.abs(r - u).astype(F32)
        lane = lax.broadcasted_iota(jnp.int32, (tq, LANES), 1)
        rows = lax.broadcasted_iota(jnp.int32, (tq, LANES), 0).astype(F32)
        ones_q = jnp.where((lane >= 3) & (lane < 9), 1.0, 0.0)
        lane_k = lax.broadcasted_iota(jnp.int32, (tk, LANES), 1)
        cols = lax.broadcasted_iota(jnp.int32, (tk, LANES), 0).astype(F32)
        ones_k = jnp.where(lane_k < 3, 1.0, 0.0)
        for side, sg in enumerate((1.0, -1.0)):
            qf = (pieces(-slope2 * sg * rows, 0) + ones_q).astype(BF16)
            q2_ref[side, 0:tq, LANES:] = qf
            q2_ref[side, tq:2 * tq, LANES:] = qf
            kfeat_ref[side] = (pieces(slope2 * sg * cols, 3) + ones_k).astype(BF16)

    lam = (jnp.exp(jnp.sum(lq1_ref[...] * lk1_ref[...], axis=-1, keepdims=True))
           - jnp.exp(jnp.sum(lq2_ref[...] * lk2_ref[...], axis=-1, keepdims=True)) + lam_init)

    def softmax(slot):
        tiles = [s_ref[slot, :, :, j * LANES:(j + 1) * LANES] for j in range(tk // LANES)]
        m_old = m_ref[...]
        m_new = jnp.maximum(m_old, jnp.max(functools.reduce(jnp.maximum, tiles), axis=-1, keepdims=True))
        alpha = jnp.exp2(m_old - m_new)
        l_new = alpha * l_ref[...]
        for j, s in enumerate(tiles):
            p = jnp.exp2(s - m_new)
            l_new = l_new + p
            p_ref[slot, :, :, j * LANES:(j + 1) * LANES] = p.astype(BF16)
        l_ref[...] = l_new
        m_ref[...] = m_new
        alpha_ref[slot] = alpha

    def q_block(qi):
        q = q_ref[0, qi * tq:(qi + 1) * tq, :]
        lane = lax.broadcasted_iota(jnp.int32, q.shape, 1)
        zero = jnp.zeros_like(q)
        for side in range(2):
            q2_ref[side, 0:tq, :LANES] = jnp.where(lane < HEAD_DK, q, zero)
            q2_ref[side, tq:2 * tq, :LANES] = jnp.where(lane >= HEAD_DK, q, zero)

        m_ref[...] = jnp.full(m_ref.shape, -jnp.inf, F32)
        l_ref[...] = jnp.zeros(l_ref.shape, F32)
        acc_ref[...] = jnp.zeros(acc_ref.shape, F32)

        def chunk_of(i):
            o = i - n_diag
            return jnp.where(i < n_diag, n_diag * qi + i, o + jnp.where(o >= n_diag * qi, n_diag, 0))

        def scores_diag(j, slot):
            ks = (n_diag * qi + j) * tk
            s = lax.dot_general(q2_ref[0, :, :LANES], k_ref[0, ks:ks + tk, :], (((1,), (1,)), ((), ())),
                                preferred_element_type=F32)
            s_ref[slot] = s.reshape(2, tq, tk) + bias_ref[:, j * tk:(j + 1) * tk][None]

        def scores(i, slot):
            c = chunk_of(i)
            ks = pl.multiple_of(c * tk, tk)
            side = (c >= n_diag * qi).astype(jnp.int32)
            sg = 1.0 - 2.0 * side.astype(F32)
            delta = (qi * tq - c * tk).astype(F32)
            dvec = jnp.full((1, LANES), -slope2 * sg * delta, F32)
            kf = kfeat_ref[side] + pieces(dvec, 6).astype(BF16)
            rhs = jnp.concatenate([k_ref[0, pl.ds(ks, tk), :], kf], axis=1)
            s = lax.dot_general(q2_ref[side], rhs, (((1,), (1,)), ((), ())), preferred_element_type=F32)
            s_ref[slot] = s.reshape(2, tq, tk)

        def values(i, slot):
            ks = pl.multiple_of(chunk_of(i) * tk, tk)
            pv = jnp.dot(p_ref[slot].reshape(2 * tq, tk), v_ref[0, pl.ds(ks, tk), :],
                         preferred_element_type=F32)
            acc_ref[...] = alpha_ref[slot] * acc_ref[...] + pv.reshape(2, tq, HEAD_DV)

        def step(i, slot):
            values(i - 1, 1 - slot)
            softmax(slot)
            scores(i + 1, 1 - slot)

        scores_diag(0, 0)
        scores_diag(1, 1)
        softmax(0)

        if n_kc > 2:
            def two_steps(n, _):
                step(2 * n + 1, 1)
                step(2 * n + 2, 0)
                return 0

            lax.fori_loop(0, (n_kc - 2) // 2, two_steps, 0)
        values(n_kc - 2, n_kc % 2)
        softmax((n_kc - 1) % 2)
        values(n_kc - 1, (n_kc - 1) % 2)

        l = jnp.sum(l_ref[...], axis=-1, keepdims=True)
        o = acc_ref[0] / l[0] - lam * (acc_ref[1] / l[1])
        o_ref[0, qi * tq:(qi + 1) * tq, :] = (_rmsnorm(o, g_ref[...]) * (1.0 - lam_init)).astype(BF16)

    for qi in range(seq // tq):
        q_block(qi)


def _attention(q, k, v, lq1, lk1, lq2, lk2, subln_g, *, lam_init, tq, tk):
    batch, seq, _ = q.shape
    slopes = jnp.asarray([2.0 ** (-8.0 * (h + 1) / N_HEADS) for h in range(N_HEADS)], F32)
    grid = (N_HEADS, batch)
    vec = lambda a: a.reshape(1, -1).astype(F32)
    small = lambda n: pl.BlockSpec((1, n), lambda h, b: (0, 0))
    head = pl.BlockSpec((1, seq, HEAD_DV), lambda h, b: (b, 0, h))
    return pl.pallas_call(
        functools.partial(_attn_kernel, seq=seq, tq=tq, tk=tk, lam_init=lam_init),
        grid=grid,
        in_specs=[
            pl.BlockSpec(memory_space=pltpu.SMEM),
            head, head, head,
            small(HEAD_DK), small(HEAD_DK), small(HEAD_DK), small(HEAD_DK), small(HEAD_DV),
        ],
        out_specs=head,
        out_shape=jax.ShapeDtypeStruct((batch, seq, N_HEADS * HEAD_DV), BF16),
        scratch_shapes=[
            pltpu.VMEM((tq, tq), F32),
            pltpu.VMEM((2, 2 * tq, 2 * HEAD_DV), BF16),
            pltpu.VMEM((2, tk, LANES), BF16),
            pltpu.VMEM((2, 2, tq, tk), F32),
            pltpu.VMEM((2, 2, tq, tk), BF16),
            pltpu.VMEM((2, 2, tq, LANES), F32),
            pltpu.VMEM((2, tq, LANES), F32),
            pltpu.VMEM((2, tq, LANES), F32),
            pltpu.VMEM((2, tq, HEAD_DV), F32),
        ],
        compiler_params=pltpu.CompilerParams(
            dimension_semantics=("arbitrary", "arbitrary"),
            vmem_limit_bytes=VMEM_LIMIT),
        name="diffattn",
    )(slopes, q, k, v, vec(lq1), vec(lk1), vec(lq2), vec(lk2), vec(subln_g))


def _sigmoid(x):
    return 1.0 / (1.0 + jnp.exp(-x))


def _softplus(x):
    return jnp.maximum(x, 0.0) + jnp.log1p(jnp.exp(-jnp.abs(x)))


def _lru_kernel(xr_ref, gate_ref, cw_ref, cb_ref, wg_ref, bg_ref, lam_ref, y_ref,
                xpad_ref, a_ref, b_ref, h_ref, *, seq, tc, n_grp):
    pad = SUBLANES

    xpad_ref[0:pad, :] = jnp.zeros((pad, LANES), F32)
    xpad_ref[pad + seq:pad + seq + pad, :] = jnp.zeros((pad, LANES), F32)
    xpad_ref[pad:pad + seq, :] = xr_ref[0]

    cw = cw_ref[...]
    cb = cb_ref[...]
    kk = _softplus(-lam_ref[0]) * (-0.5 * RG_C * LOG2E)

    def gates(c, _):
        t0 = pl.multiple_of(c * tc, tc)
        xc = cb
        for j in range(CONV_W):
            xc = xc + xpad_ref[pl.ds(t0 + pad - 2 + j, tc), :] * cw[j:j + 1, :]
        th = jnp.tanh(jnp.dot(xc.astype(BF16), wg_ref[0], preferred_element_type=F32) + bg_ref[0])
        xh = 0.5 * xc
        for d in range(2):
            t_r = th[:, (2 * d) * LANES:(2 * d + 1) * LANES]
            t_i = th[:, (2 * d + 1) * LANES:(2 * d + 2) * LANES]
            kd = kk[d:d + 1, :]
            a = jnp.exp2(kd * t_r + kd)
            m2 = jnp.maximum(1.0 - a * a, 1e-12)
            b = (m2 * lax.rsqrt(m2)) * (t_i * xh + xh)
            a_ref[d, pl.ds(t0, tc), :] = a
            b_ref[d, pl.ds(t0, tc), :] = b
        return 0

    part = seq // SUBLANES + 4
    assert (part // 4) % 2 == 1 and part % n_grp == 0
    seg = part // n_grp
    n_pad = SUBLANES * part - seq
    for d in range(2):
        a_ref[d, seq:seq + n_pad, :] = jnp.ones((n_pad, LANES), F32)
        b_ref[d, seq:seq + n_pad, :] = jnp.zeros((n_pad, LANES), F32)

    def tile(g, t):
        return pl.ds(g * seg + t, SUBLANES, stride=part)

    lax.fori_loop(0, seq // tc, gates, 0, unroll=2)

    zeros = jnp.zeros((SUBLANES, LANES), F32)
    ones = jnp.ones((SUBLANES, LANES), F32)

    def pass1(n, carry):
        out = []
        for d in range(2):
            t = n if d == 0 else seg - 1 - n
            hs, ps = carry[d]
            nh, npr = [], []
            for g in range(n_grp):
                a = a_ref[d, tile(g, t), :]
                nh.append(a * hs[g] + b_ref[d, tile(g, t), :])
                npr.append(a * ps[g])
            out.append((tuple(nh), tuple(npr)))
        return tuple(out)

    ends = lax.fori_loop(0, seg, pass1,
                         tuple(((zeros,) * n_grp, (ones,) * n_grp) for _ in range(2)), unroll=2)

    def chain_rows(e, q, reverse):
        row = lax.broadcasted_iota(jnp.int32, (SUBLANES, LANES), 0)
        edge = (SUBLANES - 1) if reverse else 0
        shift = (SUBLANES - 1) if reverse else 1
        carry = zeros
        for _ in range(SUBLANES - 1):
            carry = jnp.where(row == edge, 0.0, pltpu.roll(e + q * carry, shift, 0))
        return carry

    def entry_states(h_end, p_end, reverse):
        order = range(n_grp - 1, -1, -1) if reverse else range(n_grp)
        local, prod = {}, {}
        c, q = zeros, ones
        for g in order:
            local[g], prod[g] = c, q
            c = h_end[g] + p_end[g] * c
            q = p_end[g] * q
        row_in = chain_rows(c, q, reverse)
        return tuple(local[g] + prod[g] * row_in for g in range(n_grp))

    def pass2(n, carry):
        out = []
        for d in range(2):
            t = n if d == 0 else seg - 1 - n
            nh = []
            for g in range(n_grp):
                hcur = a_ref[d, tile(g, t), :] * carry[d][g] + b_ref[d, tile(g, t), :]
                h_ref[d, tile(g, t), :] = hcur
                nh.append(hcur)
            out.append(tuple(nh))
        return tuple(out)

    lax.fori_loop(0, seg, pass2,
                  tuple(entry_states(ends[d][0], ends[d][1], d == 1) for d in range(2)), unroll=2)

    c1 = math.sqrt(2.0 / math.pi)

    def finish(c, _):
        t0 = pl.multiple_of(c * tc, tc)
        hsum = h_ref[0, pl.ds(t0, tc), :] + h_ref[1, pl.ds(t0, tc), :]
        x = gate_ref[0, pl.ds(t0, tc), :]
        th = jnp.tanh(x * ((x * x) * (c1 * 0.044715) + c1))
        xh = 0.5 * x
        y_ref[0, pl.ds(t0, tc), :] = (hsum * (xh * th + xh)).astype(BF16)
        return 0

    lax.fori_loop(0, seq // tc, finish, 0)


def _lru_gate_weights(w_rg, b_rg, w_ig, b_ig):
    blk = w_rg.shape[-1]
    per = LANES // blk
    n_groups = N_LRU_BLOCKS // per

    def dense(w):
        w = w.reshape(n_groups, per, blk, blk)
        eye = jnp.eye(per, dtype=w.dtype)
        return jnp.einsum('gpcd,pq->gpcqd', w, eye).reshape(n_groups, LANES, LANES)

    wg = jnp.concatenate([dense(w_rg[0]), dense(w_ig[0]), dense(w_rg[1]), dense(w_ig[1])], axis=-1)
    bias = lambda v: v.reshape(n_groups, 1, LANES)
    bg = jnp.concatenate([bias(b_rg[0]), bias(b_ig[0]), bias(b_rg[1]), bias(b_ig[1])], axis=-1)
    return (0.5 * wg).astype(BF16), (0.5 * bg).astype(F32)


def _rglru(xr, gate, conv_w, conv_b, wg, bg, lru_lambda, *, tc, n_grp):
    batch, seq, lru_w = xr.shape
    n_groups = lru_w // LANES
    lam = lru_lambda.reshape(2, n_groups, LANES).transpose(1, 0, 2)
    grid = (batch, n_groups)
    tok = pl.BlockSpec((1, seq, LANES), lambda b, j: (b, 0, j))
    return pl.pallas_call(
        functools.partial(_lru_kernel, seq=seq, tc=tc, n_grp=n_grp),
        grid=grid,
        in_specs=[
            tok, tok,
            pl.BlockSpec((CONV_W, LANES), lambda b, j: (0, j)),
            pl.BlockSpec((1, LANES), lambda b, j: (0, j)),
            pl.BlockSpec((1, LANES, 4 * LANES), lambda b, j: (j, 0, 0)),
            pl.BlockSpec((1, 1, 4 * LANES), lambda b, j: (j, 0, 0)),
            pl.BlockSpec((1, 2, LANES), lambda b, j: (j, 0, 0)),
        ],
        out_specs=tok,
        out_shape=jax.ShapeDtypeStruct((batch, seq, lru_w), BF16),
        scratch_shapes=[
            pltpu.VMEM((seq + 2 * SUBLANES, LANES), F32),
            pltpu.VMEM((2, seq + 4 * SUBLANES, LANES), F32),
            pltpu.VMEM((2, seq + 4 * SUBLANES, LANES), F32),
            pltpu.VMEM((2, seq + 4 * SUBLANES, LANES), F32),
        ],
        compiler_params=pltpu.CompilerParams(
            dimension_semantics=("arbitrary", "arbitrary"), vmem_limit_bytes=VMEM_LIMIT),
        name="rglru",
    )(xr, gate, conv_w, conv_b.reshape(1, lru_w), wg, bg, lam)


def _ffn_kernel(x_ref, o_ref, y_ref, wo_ref, g2_ref, wg_ref, wu_ref, wd_ref, g3_ref, out_ref, *, att_w):
    x1 = (x_ref[...]
          + jnp.dot(o_ref[...], wo_ref[0:att_w, :], preferred_element_type=F32)
          + jnp.dot(y_ref[...], wo_ref[att_w:, :], preferred_element_type=F32))
    h2 = _rmsnorm(x1, g2_ref[...]).astype(BF16)
    gt = jnp.dot(h2, wg_ref[...], preferred_element_type=F32)
    up = jnp.dot(h2, wu_ref[...], preferred_element_type=F32)
    act = ((gt * _sigmoid(gt)) * up).astype(BF16)
    x2 = x1 + jnp.dot(act, wd_ref[...], preferred_element_type=F32)
    out_ref[...] = _rmsnorm(x2, g3_ref[...])


def _out_ffn(x2d, o2d, y2d, w_out, norm_ffn, w_gate, w_up, w_down, norm_final, *, tm):
    n_tok, d_model = x2d.shape
    att_w = o2d.shape[1]
    lru_w = y2d.shape[1]
    row = lambda i: (i, 0)
    const = lambda i: (0, 0)
    resident = lambda a: pl.BlockSpec(a.shape, const, pipeline_mode=pl.Buffered(1))
    g2 = norm_ffn.reshape(1, d_model)
    g3 = norm_final.reshape(1, d_model)
    return pl.pallas_call(
        functools.partial(_ffn_kernel, att_w=att_w),
        grid=(n_tok // tm,),
        in_specs=[
            pl.BlockSpec((tm, d_model), row),
            pl.BlockSpec((tm, att_w), row),
            pl.BlockSpec((tm, lru_w), row),
            resident(w_out), resident(g2), resident(w_gate), resident(w_up), resident(w_down),
            resident(g3),
        ],
        out_specs=pl.BlockSpec((tm, d_model), row),
        out_shape=jax.ShapeDtypeStruct((n_tok, d_model), F32),
        compiler_params=pltpu.CompilerParams(
            dimension_semantics=("arbitrary",), vmem_limit_bytes=VMEM_LIMIT),
        name="out_ffn",
    )(x2d, o2d, y2d, w_out, g2, w_gate, w_up, w_down, g3)


def _trunk(x, p, *, tm_in, tq, tk, tc, n_grp, tm_ffn):
    batch, seq, d_model = x.shape
    x2d = x.reshape(batch * seq, d_model)
    depth = p["w_in"].shape[0]
    for l in range(depth):
        last = l == depth - 1
        lam_init = 0.8 - 0.6 * math.exp(-0.3 * l)
        q, k, v, xr, gate = _inproj(x2d, p["norm_mix"][l], p["w_in"][l].astype(BF16), tm=tm_in)
        att_w = v.shape[1]
        lru_w = xr.shape[1]
        shp = lambda a: a.reshape(batch, seq, a.shape[1])
        o = _attention(shp(q), shp(k), shp(v), p["lambda_q1"][l], p["lambda_k1"][l],
                       p["lambda_q2"][l], p["lambda_k2"][l], p["subln_g"][l],
                       lam_init=lam_init, tq=tq, tk=tk)
        wg, bg = _lru_gate_weights(p["w_rg"][l], p["b_rg"][l], p["w_ig"][l], p["b_ig"][l])
        y = _rglru(shp(xr), shp(gate), p["conv_w"][l], p["conv_b"][l], wg, bg, p["lru_lambda"][l],
                   tc=tc, n_grp=n_grp)
        assert last, "only the last layer's fused final norm is implemented"
        x2d = _out_ffn(x2d, o.reshape(batch * seq, att_w), y.reshape(batch * seq, lru_w),
                       p["w_out"][l].astype(BF16), p["norm_ffn"][l], p["w_gate"][l].astype(BF16),
                       p["w_up"][l].astype(BF16), p["w_down"][l].astype(BF16), p["norm_final"],
                       tm=tm_ffn)
    return x2d.reshape(batch, seq, d_model)


def _tiles(seq):
    return dict(tm_in=512, tq=min(1024, seq), tk=min(512, seq), tc=min(256, seq // SUBLANES), n_grp=4,
                tm_ffn=512)


def kernel(x_prompt, x_sample, norm_mix, w_in, conv_w, conv_b, w_rg, b_rg, w_ig, b_ig, lru_lambda, lambda_q1, lambda_k1, lambda_q2, lambda_k2, subln_g, w_out, norm_ffn, w_gate, w_up, w_down, norm_final):
    p = dict(norm_mix=norm_mix, w_in=w_in, conv_w=conv_w, conv_b=conv_b, w_rg=w_rg, b_rg=b_rg,
             w_ig=w_ig, b_ig=b_ig, lru_lambda=lru_lambda, lambda_q1=lambda_q1, lambda_k1=lambda_k1,
             lambda_q2=lambda_q2, lambda_k2=lambda_k2, subln_g=subln_g, w_out=w_out,
             norm_ffn=norm_ffn, w_gate=w_gate, w_up=w_up, w_down=w_down, norm_final=norm_final)
    y_prompt = _trunk(x_prompt, p, **_tiles(x_prompt.shape[1]))
    y_sample = _trunk(x_sample, p, **_tiles(x_sample.shape[1]))
    return (y_prompt, y_sample)
```

```python
import functools
import math

import jax
import jax.numpy as jnp
from jax import lax
from jax.experimental import pallas as pl
from jax.experimental.pallas import tpu as pltpu

F32 = jnp.float32
BF16 = jnp.bfloat16

N_HEADS = 4
HEAD_DV = 128
HEAD_DK = 64
N_LRU_BLOCKS = 8
CONV_W = 4
RG_C = 8.0
NORM_EPS = 1e-6
LOG2E = math.log2(math.e)
LANES = 128
SUBLANES = 8
VMEM_LIMIT = 56 * 1024 * 1024


def _rmsnorm(x, g):
    ms = jnp.mean(x * x, axis=-1, keepdims=True)
    return (x * lax.rsqrt(ms + NORM_EPS)) * g


def _inproj_kernel(x_ref, g_ref, w_ref, q_ref, k_ref, v_ref, xr_ref, gate_ref, *, qk_w, att_w, lru_w):
    h = _rmsnorm(x_ref[...], g_ref[...]).astype(BF16)

    def proj(lo, width):
        return jnp.dot(h, w_ref[:, lo:lo + width], preferred_element_type=F32)

    q_ref[...] = (proj(0, qk_w) * (LOG2E / math.sqrt(HEAD_DK))).astype(BF16)
    k_ref[...] = proj(qk_w, qk_w).astype(BF16)
    v_ref[...] = proj(2 * qk_w, att_w).astype(BF16)
    xr_ref[...] = proj(2 * qk_w + att_w, lru_w)
    gate_ref[...] = proj(2 * qk_w + att_w + lru_w, lru_w)


def _inproj(x2d, norm_g, w_in_bf16, *, tm):
    n_tok, d_model = x2d.shape
    qk_w = N_HEADS * 2 * HEAD_DK
    att_w = N_HEADS * HEAD_DV
    lru_w = (w_in_bf16.shape[1] - 2 * qk_w - att_w) // 2
    grid = (n_tok // tm,)
    row = lambda i: (i, 0)
    const = lambda i: (0, 0)
    return pl.pallas_call(
        functools.partial(_inproj_kernel, qk_w=qk_w, att_w=att_w, lru_w=lru_w),
        grid=grid,
        in_specs=[
            pl.BlockSpec((tm, d_model), row),
            pl.BlockSpec((1, d_model), const),
            pl.BlockSpec(w_in_bf16.shape, const, pipeline_mode=pl.Buffered(1)),
        ],
        out_specs=[
            pl.BlockSpec((tm, qk_w), row),
            pl.BlockSpec((tm, qk_w), row),
            pl.BlockSpec((tm, att_w), row),
            pl.BlockSpec((tm, lru_w), row),
            pl.BlockSpec((tm, lru_w), row),
        ],
        out_shape=[
            jax.ShapeDtypeStruct((n_tok, qk_w), BF16),
            jax.ShapeDtypeStruct((n_tok, qk_w), BF16),
            jax.ShapeDtypeStruct((n_tok, att_w), BF16),
            jax.ShapeDtypeStruct((n_tok, lru_w), F32),
            jax.ShapeDtypeStruct((n_tok, lru_w), F32),
        ],
        compiler_params=pltpu.CompilerParams(
            dimension_semantics=("arbitrary",), vmem_limit_bytes=VMEM_LIMIT),
        name="inproj",
    )(x2d, norm_g.reshape(1, d_model), w_in_bf16)


def _attn_kernel(slopes_ref, q_ref, k_ref, v_ref, lq1_ref, lk1_ref, lq2_ref, lk2_ref, g_ref,
                 o_ref, bias_ref, q2_ref, kfeat_ref, s_ref, p_ref, alpha_ref, m_ref, l_ref, acc_ref,
                 *, seq, tq, tk, lam_init):
    h = pl.program_id(0)
    b = pl.program_id(1)
    n_kc = seq // tk
    n_diag = tq // tk
    assert n_diag == 2 and n_kc >= n_diag and n_kc % 2 == 0
    slope2 = LOG2E * slopes_ref[h]

    def split3(x):
        hi = x.astype(BF16).astype(F32)
        mid = (x - hi).astype(BF16).astype(F32)
        return hi, mid, x - hi - mid

    def pieces(x, first):
        lane = lax.broadcasted_iota(jnp.int32, x.shape, x.ndim - 1)
        hi, mid, lo = split3(x)
        return jnp.where(lane == first, hi, jnp.where(lane == first + 1, mid,
                                                      jnp.where(lane == first + 2, lo, 0.0)))

    @pl.when(b == 0)
    def _():
        r = lax.broadcasted_iota(jnp.int32, (tq, tq), 0)
        u = lax.broadcasted_iota(jnp.int32, (tq, tq), 1)
        bias_ref[...] = -slope2 * jnp.abs(r - u).astype(F32)
        lane = lax.broadcasted_iota(jnp.int32, (tq, LANES), 1)
        rows = lax.broadcasted_iota(jnp.int32, (tq, LANES), 0).astype(F32)
        ones_q = jnp.where((lane >= 3) & (lane < 9), 1.0, 0.0)
        lane_k = lax.broadcasted_iota(jnp.int32, (tk, LANES), 1)
        cols = lax.broadcasted_iota(jnp.int32, (tk, LANES), 0).astype(F32)
        ones_k = jnp.where(lane_k < 3, 1.0, 0.0)
        for side, sg in enumerate((1.0, -1.0)):
            qf = (pieces(-slope2 * sg * rows, 0) + ones_q).astype(BF16)
            q2_ref[side, 0:tq, LANES:] = qf
            q2_ref[side, tq:2 * tq, LANES:] = qf
            kfeat_ref[side] = (pieces(slope2 * sg * cols, 3) + ones_k).astype(BF16)

    lam = (jnp.exp(jnp.sum(lq1_ref[...] * lk1_ref[...], axis=-1, keepdims=True))
           - jnp.exp(jnp.sum(lq2_ref[...] * lk2_ref[...], axis=-1, keepdims=True)) + lam_init)

    def softmax(slot):
        tiles = [s_ref[slot, :, :, j * LANES:(j + 1) * LANES] for j in range(tk // LANES)]
        m_old = m_ref[...]
        m_new = jnp.maximum(m_old, jnp.max(functools.reduce(jnp.maximum, tiles), axis=-1, keepdims=True))
        alpha = jnp.exp2(m_old - m_new)
        l_new = alpha * l_ref[...]
        for j, s in enumerate(tiles):
            p = jnp.exp2(s - m_new)
            l_new = l_new + p
            p_ref[slot, :, :, j * LANES:(j + 1) * LANES] = p.astype(BF16)
        l_ref[...] = l_new
        m_ref[...] = m_new
        alpha_ref[slot] = alpha

    def q_block(qi):
        q = q_ref[0, qi * tq:(qi + 1) * tq, :]
        lane = lax.broadcasted_iota(jnp.int32, q.shape, 1)
        zero = jnp.zeros_like(q)
        for side in range(2):
            q2_ref[side, 0:tq, :LANES] = jnp.where(lane < HEAD_DK, q, zero)
            q2_ref[side, tq:2 * tq, :LANES] = jnp.where(lane >= HEAD_DK, q, zero)

        m_ref[...] = jnp.full(m_ref.shape, -jnp.inf, F32)
        l_ref[...] = jnp.zeros(l_ref.shape, F32)
        acc_ref[...] = jnp.zeros(acc_ref.shape, F32)

        def chunk_of(i):
            o = i - n_diag
            return jnp.where(i < n_diag, n_diag * qi + i, o + jnp.where(o >= n_diag * qi, n_diag, 0))

        def scores_diag(j, slot):
            ks = (n_diag * qi + j) * tk
            s = lax.dot_general(q2_ref[0, :, :LANES], k_ref[0, ks:ks + tk, :], (((1,), (1,)), ((), ())),
                                preferred_element_type=F32)
            s_ref[slot] = s.reshape(2, tq, tk) + bias_ref[:, j * tk:(j + 1) * tk][None]

        def scores(i, slot):
            c = chunk_of(i)
            ks = pl.multiple_of(c * tk, tk)
            side = (c >= n_diag * qi).astype(jnp.int32)
            sg = 1.0 - 2.0 * side.astype(F32)
            delta = (qi * tq - c * tk).astype(F32)
            dvec = jnp.full((1, LANES), -slope2 * sg * delta, F32)
            kf = kfeat_ref[side] + pieces(dvec, 6).astype(BF16)
            rhs = jnp.concatenate([k_ref[0, pl.ds(ks, tk), :], kf], axis=1)
            s = lax.dot_general(q2_ref[side], rhs, (((1,), (1,)), ((), ())), preferred_element_type=F32)
            s_ref[slot] = s.reshape(2, tq, tk)

        def values(i, slot):
            ks = pl.multiple_of(chunk_of(i) * tk, tk)
            pv = jnp.dot(p_ref[slot].reshape(2 * tq, tk), v_ref[0, pl.ds(ks, tk), :],
                         preferred_element_type=F32)
            acc_ref[...] = alpha_ref[slot] * acc_ref[...] + pv.reshape(2, tq, HEAD_DV)

        def step(i, slot):
            values(i - 1, 1 - slot)
            softmax(slot)
            scores(i + 1, 1 - slot)

        scores_diag(0, 0)
        scores_diag(1, 1)
        softmax(0)

        if n_kc > 2:
            def two_steps(n, _):
                step(2 * n + 1, 1)
                step(2 * n + 2, 0)
                return 0

            n_pairs = (n_kc - 2) // 2
            lax.fori_loop(0, jnp.minimum(h + n_pairs, n_pairs), two_steps, 0)
        values(n_kc - 2, n_kc % 2)
        softmax((n_kc - 1) % 2)
        values(n_kc - 1, (n_kc - 1) % 2)

        l = jnp.sum(l_ref[...], axis=-1, keepdims=True)
        o = acc_ref[0] / l[0] - lam * (acc_ref[1] / l[1])
        o_ref[0, qi * tq:(qi + 1) * tq, :] = (_rmsnorm(o, g_ref[...]) * (1.0 - lam_init)).astype(BF16)

    for qi in range(seq // tq):
        q_block(qi)


def _attention(q, k, v, lq1, lk1, lq2, lk2, subln_g, *, lam_init, tq, tk):
    batch, seq, _ = q.shape
    slopes = jnp.asarray([2.0 ** (-8.0 * (h + 1) / N_HEADS) for h in range(N_HEADS)], F32)
    grid = (N_HEADS, batch)
    vec = lambda a: a.reshape(1, -1).astype(F32)
    small = lambda n: pl.BlockSpec((1, n), lambda h, b: (0, 0))
    head = pl.BlockSpec((1, seq, HEAD_DV), lambda h, b: (b, 0, h))
    return pl.pallas_call(
        functools.partial(_attn_kernel, seq=seq, tq=tq, tk=tk, lam_init=lam_init),
        grid=grid,
        in_specs=[
            pl.BlockSpec(memory_space=pltpu.SMEM),
            head, head, head,
            small(HEAD_DK), small(HEAD_DK), small(HEAD_DK), small(HEAD_DK), small(HEAD_DV),
        ],
        out_specs=head,
        out_shape=jax.ShapeDtypeStruct((batch, seq, N_HEADS * HEAD_DV), BF16),
        scratch_shapes=[
            pltpu.VMEM((tq, tq), F32),
            pltpu.VMEM((2, 2 * tq, 2 * HEAD_DV), BF16),
            pltpu.VMEM((2, tk, LANES), BF16),
            pltpu.VMEM((2, 2, tq, tk), F32),
            pltpu.VMEM((2, 2, tq, tk), BF16),
            pltpu.VMEM((2, 2, tq, LANES), F32),
            pltpu.VMEM((2, tq, LANES), F32),
            pltpu.VMEM((2, tq, LANES), F32),
            pltpu.VMEM((2, tq, HEAD_DV), F32),
        ],
        compiler_params=pltpu.CompilerParams(
            dimension_semantics=("arbitrary", "arbitrary"),
            vmem_limit_bytes=VMEM_LIMIT),
        name="diffattn",
    )(slopes, q, k, v, vec(lq1), vec(lk1), vec(lq2), vec(lk2), vec(subln_g))


def _sigmoid(x):
    return 1.0 / (1.0 + jnp.exp(-x))


def _softplus(x):
    return jnp.maximum(x, 0.0) + jnp.log1p(jnp.exp(-jnp.abs(x)))


def _lru_kernel(xr_ref, gate_ref, cw_ref, cb_ref, wg_ref, bg_ref, lam_ref, y_ref,
                xpad_ref, a_ref, b_ref, h_ref, *, seq, tc, n_grp):
    pad = SUBLANES

    xpad_ref[0:pad, :] = jnp.zeros((pad, LANES), F32)
    xpad_ref[pad + seq:pad + seq + pad, :] = jnp.zeros((pad, LANES), F32)
    xpad_ref[pad:pad + seq, :] = xr_ref[0]

    cw = cw_ref[...]
    cb = cb_ref[...]
    kk = _softplus(-lam_ref[0]) * (-0.5 * RG_C * LOG2E)

    def gates(c, _):
        t0 = pl.multiple_of(c * tc, tc)
        xc = cb
        for j in range(CONV_W):
            xc = xc + xpad_ref[pl.ds(t0 + pad - 2 + j, tc), :] * cw[j:j + 1, :]
        th = jnp.tanh(jnp.dot(xc.astype(BF16), wg_ref[0], preferred_element_type=F32) + bg_ref[0])
        xh = 0.5 * xc
        for d in range(2):
            t_r = th[:, (2 * d) * LANES:(2 * d + 1) * LANES]
            t_i = th[:, (2 * d + 1) * LANES:(2 * d + 2) * LANES]
            kd = kk[d:d + 1, :]
            a = jnp.exp2(kd * t_r + kd)
            m2 = jnp.maximum(1.0 - a * a, 1e-12)
            b = (m2 * lax.rsqrt(m2)) * (t_i * xh + xh)
            a_ref[d, pl.ds(t0, tc), :] = a
            b_ref[d, pl.ds(t0, tc), :] = b
        return 0

    part = seq // SUBLANES + 4
    assert (part // 4) % 2 == 1 and part % n_grp == 0
    seg = part // n_grp
    n_pad = SUBLANES * part - seq
    for d in range(2):
        a_ref[d, seq:seq + n_pad, :] = jnp.ones((n_pad, LANES), F32)
        b_ref[d, seq:seq + n_pad, :] = jnp.zeros((n_pad, LANES), F32)

    def tile(g, t):
        return pl.ds(g * seg + t, SUBLANES, stride=part)

    lax.fori_loop(0, seq // tc, gates, 0, unroll=2)

    zeros = jnp.zeros((SUBLANES, LANES), F32)
    ones = jnp.ones((SUBLANES, LANES), F32)

    def pass1(n, carry):
        out = []
        for d in range(2):
            t = n if d == 0 else seg - 1 - n
            hs, ps = carry[d]
            nh, npr = [], []
            for g in range(n_grp):
                a = a_ref[d, tile(g, t), :]
                nh.append(a * hs[g] + b_ref[d, tile(g, t), :])
                npr.append(a * ps[g])
            out.append((tuple(nh), tuple(npr)))
        return tuple(out)

    ends = lax.fori_loop(0, seg, pass1,
                         tuple(((zeros,) * n_grp, (ones,) * n_grp) for _ in range(2)), unroll=2)

    def chain_rows(e, q, reverse):
        row = lax.broadcasted_iota(jnp.int32, (SUBLANES, LANES), 0)
        edge = (SUBLANES - 1) if reverse else 0
        shift = (SUBLANES - 1) if reverse else 1
        carry = zeros
        for _ in range(SUBLANES - 1):
            carry = jnp.where(row == edge, 0.0, pltpu.roll(e + q * carry, shift, 0))
        return carry

    def entry_states(h_end, p_end, reverse):
        order = range(n_grp - 1, -1, -1) if reverse else range(n_grp)
        local, prod = {}, {}
        c, q = zeros, ones
        for g in order:
            local[g], prod[g] = c, q
            c = h_end[g] + p_end[g] * c
            q = p_end[g] * q
        row_in = chain_rows(c, q, reverse)
        return tuple(local[g] + prod[g] * row_in for g in range(n_grp))

    def pass2(n, carry):
        out = []
        for d in range(2):
            t = n if d == 0 else seg - 1 - n
            nh = []
            for g in range(n_grp):
                hcur = a_ref[d, tile(g, t), :] * carry[d][g] + b_ref[d, tile(g, t), :]
                h_ref[d, tile(g, t), :] = hcur
                nh.append(hcur)
            out.append(tuple(nh))
        return tuple(out)

    lax.fori_loop(0, seg, pass2,
                  tuple(entry_states(ends[d][0], ends[d][1], d == 1) for d in range(2)), unroll=2)

    c1 = math.sqrt(2.0 / math.pi)

    def finish(c, _):
        t0 = pl.multiple_of(c * tc, tc)
        hsum = h_ref[0, pl.ds(t0, tc), :] + h_ref[1, pl.ds(t0, tc), :]
        x = gate_ref[0, pl.ds(t0, tc), :]
        th = jnp.tanh(x * ((x * x) * (c1 * 0.044715) + c1))
        xh = 0.5 * x
        y_ref[0, pl.ds(t0, tc), :] = (hsum * (xh * th + xh)).astype(BF16)
        return 0

    lax.fori_loop(0, seq // tc, finish, 0)


def _lru_gate_weights(w_rg, b_rg, w_ig, b_ig):
    blk = w_rg.shape[-1]
    per = LANES // blk
    n_groups = N_LRU_BLOCKS // per

    def dense(w):
        w = w.reshape(n_groups, per, blk, blk)
        eye = jnp.eye(per, dtype=w.dtype)
        return jnp.einsum('gpcd,pq->gpcqd', w, eye).reshape(n_groups, LANES, LANES)

    wg = jnp.concatenate([dense(w_rg[0]), dense(w_ig[0]), dense(w_rg[1]), dense(w_ig[1])], axis=-1)
    bias = lambda v: v.reshape(n_groups, 1, LANES)
    bg = jnp.concatenate([bias(b_rg[0]), bias(b_ig[0]), bias(b_rg[1]), bias(b_ig[1])], axis=-1)
    return (0.5 * wg).astype(BF16), (0.5 * bg).astype(F32)


def _rglru(xr, gate, conv_w, conv_b, wg, bg, lru_lambda, *, tc, n_grp):
    batch, seq, lru_w = xr.shape
    n_groups = lru_w // LANES
    lam = lru_lambda.reshape(2, n_groups, LANES).transpose(1, 0, 2)
    grid = (batch, n_groups)
    tok = pl.BlockSpec((1, seq, LANES), lambda b, j: (b, 0, j))
    return pl.pallas_call(
        functools.partial(_lru_kernel, seq=seq, tc=tc, n_grp=n_grp),
        grid=grid,
        in_specs=[
            tok, tok,
            pl.BlockSpec((CONV_W, LANES), lambda b, j: (0, j)),
            pl.BlockSpec((1, LANES), lambda b, j: (0, j)),
            pl.BlockSpec((1, LANES, 4 * LANES), lambda b, j: (j, 0, 0)),
            pl.BlockSpec((1, 1, 4 * LANES), lambda b, j: (j, 0, 0)),
            pl.BlockSpec((1, 2, LANES), lambda b, j: (j, 0, 0)),
        ],
        out_specs=tok,
        out_shape=jax.ShapeDtypeStruct((batch, seq, lru_w), BF16),
        scratch_shapes=[
            pltpu.VMEM((seq + 2 * SUBLANES, LANES), F32),
            pltpu.VMEM((2, seq + 4 * SUBLANES, LANES), F32),
            pltpu.VMEM((2, seq + 4 * SUBLANES, LANES), F32),
            pltpu.VMEM((2, seq + 4 * SUBLANES, LANES), F32),
        ],
        compiler_params=pltpu.CompilerParams(
            dimension_semantics=("arbitrary", "arbitrary"), vmem_limit_bytes=VMEM_LIMIT),
        name="rglru",
    )(xr, gate, conv_w, conv_b.reshape(1, lru_w), wg, bg, lam)


def _ffn_kernel(x_ref, o_ref, y_ref, wo_ref, g2_ref, wg_ref, wu_ref, wd_ref, g3_ref, out_ref, *, att_w):
    x1 = (x_ref[...]
          + jnp.dot(o_ref[...], wo_ref[0:att_w, :], preferred_element_type=F32)
          + jnp.dot(y_ref[...], wo_ref[att_w:, :], preferred_element_type=F32))
    h2 = _rmsnorm(x1, g2_ref[...]).astype(BF16)
    gt = jnp.dot(h2, wg_ref[...], preferred_element_type=F32)
    up = jnp.dot(h2, wu_ref[...], preferred_element_type=F32)
    act = ((gt * _sigmoid(gt)) * up).astype(BF16)
    x2 = x1 + jnp.dot(act, wd_ref[...], preferred_element_type=F32)
    out_ref[...] = _rmsnorm(x2, g3_ref[...])


def _out_ffn(x2d, o2d, y2d, w_out, norm_ffn, w_gate, w_up, w_down, norm_final, *, tm):
    n_tok, d_model = x2d.shape
    att_w = o2d.shape[1]
    lru_w = y2d.shape[1]
    row = lambda i: (i, 0)
    const = lambda i: (0, 0)
    resident = lambda a: pl.BlockSpec(a.shape, const, pipeline_mode=pl.Buffered(1))
    g2 = norm_ffn.reshape(1, d_model)
    g3 = norm_final.reshape(1, d_model)
    return pl.pallas_call(
        functools.partial(_ffn_kernel, att_w=att_w),
        grid=(n_tok // tm,),
        in_specs=[
            pl.BlockSpec((tm, d_model), row),
            pl.BlockSpec((tm, att_w), row),
            pl.BlockSpec((tm, lru_w), row),
            resident(w_out), resident(g2), resident(w_gate), resident(w_up), resident(w_down),
            resident(g3),
        ],
        out_specs=pl.BlockSpec((tm, d_model), row),
        out_shape=jax.ShapeDtypeStruct((n_tok, d_model), F32),
        compiler_params=pltpu.CompilerParams(
            dimension_semantics=("arbitrary",), vmem_limit_bytes=VMEM_LIMIT),
        name="out_ffn",
    )(x2d, o2d, y2d, w_out, g2, w_gate, w_up, w_down, g3)


def _trunk(x, p, *, tm_in, tq, tk, tc, n_grp, tm_ffn):
    batch, seq, d_model = x.shape
    x2d = x.reshape(batch * seq, d_model)
    depth = p["w_in"].shape[0]
    for l in range(depth):
        last = l == depth - 1
        lam_init = 0.8 - 0.6 * math.exp(-0.3 * l)
        q, k, v, xr, gate = _inproj(x2d, p["norm_mix"][l], p["w_in"][l].astype(BF16), tm=tm_in)
        att_w = v.shape[1]
        lru_w = xr.shape[1]
        shp = lambda a: a.reshape(batch, seq, a.shape[1])
        o = _attention(shp(q), shp(k), shp(v), p["lambda_q1"][l], p["lambda_k1"][l],
                       p["lambda_q2"][l], p["lambda_k2"][l], p["subln_g"][l],
                       lam_init=lam_init, tq=tq, tk=tk)
        wg, bg = _lru_gate_weights(p["w_rg"][l], p["b_rg"][l], p["w_ig"][l], p["b_ig"][l])
        y = _rglru(shp(xr), shp(gate), p["conv_w"][l], p["conv_b"][l], wg, bg, p["lru_lambda"][l],
                   tc=tc, n_grp=n_grp)
        assert last, "only the last layer's fused final norm is implemented"
        x2d = _out_ffn(x2d, o.reshape(batch * seq, att_w), y.reshape(batch * seq, lru_w),
                       p["w_out"][l].astype(BF16), p["norm_ffn"][l], p["w_gate"][l].astype(BF16),
                       p["w_up"][l].astype(BF16), p["w_down"][l].astype(BF16), p["norm_final"],
                       tm=tm_ffn)
    return x2d.reshape(batch, seq, d_model)


def _tiles(seq):
    return dict(tm_in=512, tq=min(1024, seq), tk=min(512, seq), tc=min(256, seq // SUBLANES), n_grp=4,
                tm_ffn=512)


def kernel(x_prompt, x_sample, norm_mix, w_in, conv_w, conv_b, w_rg, b_rg, w_ig, b_ig, lru_lambda, lambda_q1, lambda_k1, lambda_q2, lambda_k2, subln_g, w_out, norm_ffn, w_gate, w_up, w_down, norm_final):
    p = dict(norm_mix=norm_mix, w_in=w_in, conv_w=conv_w, conv_b=conv_b, w_rg=w_rg, b_rg=b_rg,
             w_ig=w_ig, b_ig=b_ig, lru_lambda=lru_lambda, lambda_q1=lambda_q1, lambda_k1=lambda_k1,
             lambda_q2=lambda_q2, lambda_k2=lambda_k2, subln_g=subln_g, w_out=w_out,
             norm_ffn=norm_ffn, w_gate=w_gate, w_up=w_up, w_down=w_down, norm_final=norm_final)
    y_prompt = _trunk(x_prompt, p, **_tiles(x_prompt.shape[1]))
    y_sample = _trunk(x_sample, p, **_tiles(x_sample.shape[1]))
    return (y_prompt, y_sample)
```

```python
import functools
import math

import jax
import jax.numpy as jnp
from jax import lax
from jax.experimental import pallas as pl
from jax.experimental.pallas import tpu as pltpu

F32 = jnp.float32
BF16 = jnp.bfloat16

N_HEADS = 4
HEAD_DV = 128
HEAD_DK = 64
N_LRU_BLOCKS = 8
CONV_W = 4
RG_C = 8.0
NORM_EPS = 1e-6
LOG2E = math.log2(math.e)
LANES = 128
SUBLANES = 8
VMEM_LIMIT = 56 * 1024 * 1024


def _rmsnorm(x, g):
    ms = jnp.mean(x * x, axis=-1, keepdims=True)
    return (x * lax.rsqrt(ms + NORM_EPS)) * g


def _inproj_kernel(x_ref, g_ref, w_ref, q_ref, k_ref, v_ref, xr_ref, gate_ref, *, qk_w, att_w, lru_w):
    h = _rmsnorm(x_ref[...], g_ref[...]).astype(BF16)

    def proj(lo, width):
        return jnp.dot(h, w_ref[:, lo:lo + width], preferred_element_type=F32)

    q_ref[...] = (proj(0, qk_w) * (LOG2E / math.sqrt(HEAD_DK))).astype(BF16)
    k_ref[...] = proj(qk_w, qk_w).astype(BF16)
    v_ref[...] = proj(2 * qk_w, att_w).astype(BF16)
    xr_ref[...] = proj(2 * qk_w + att_w, lru_w)
    gate_ref[...] = proj(2 * qk_w + att_w + lru_w, lru_w)


def _inproj(x2d, norm_g, w_in_bf16, *, tm):
    n_tok, d_model = x2d.shape
    qk_w = N_HEADS * 2 * HEAD_DK
    att_w = N_HEADS * HEAD_DV
    lru_w = (w_in_bf16.shape[1] - 2 * qk_w - att_w) // 2
    grid = (n_tok // tm,)
    row = lambda i: (i, 0)
    const = lambda i: (0, 0)
    return pl.pallas_call(
        functools.partial(_inproj_kernel, qk_w=qk_w, att_w=att_w, lru_w=lru_w),
        grid=grid,
        in_specs=[
            pl.BlockSpec((tm, d_model), row),
            pl.BlockSpec((1, d_model), const),
            pl.BlockSpec(w_in_bf16.shape, const, pipeline_mode=pl.Buffered(1)),
        ],
        out_specs=[
            pl.BlockSpec((tm, qk_w), row),
            pl.BlockSpec((tm, qk_w), row),
            pl.BlockSpec((tm, att_w), row),
            pl.BlockSpec((tm, lru_w), row),
            pl.BlockSpec((tm, lru_w), row),
        ],
        out_shape=[
            jax.ShapeDtypeStruct((n_tok, qk_w), BF16),
            jax.ShapeDtypeStruct((n_tok, qk_w), BF16),
            jax.ShapeDtypeStruct((n_tok, att_w), BF16),
            jax.ShapeDtypeStruct((n_tok, lru_w), F32),
            jax.ShapeDtypeStruct((n_tok, lru_w), F32),
        ],
        compiler_params=pltpu.CompilerParams(
            dimension_semantics=("arbitrary",), vmem_limit_bytes=VMEM_LIMIT),
        name="inproj",
    )(x2d, norm_g.reshape(1, d_model), w_in_bf16)


def _attn_kernel(slopes_ref, q_ref, k_ref, v_ref, lq1_ref, lk1_ref, lq2_ref, lk2_ref, g_ref,
                 o_ref, bias_ref, q2_ref, kfeat_ref, s_ref, p_ref, alpha_ref, m_ref, l_ref, acc_ref,
                 *, seq, tq, tk, lam_init):
    h = pl.program_id(0)
    b = pl.program_id(1)
    n_kc = seq // tk
    n_diag = tq // tk
    assert n_diag == 2 and n_kc >= n_diag and n_kc % 2 == 0
    slope2 = LOG2E * slopes_ref[h]

    def split3(x):
        hi = x.astype(BF16).astype(F32)
        mid = (x - hi).astype(BF16).astype(F32)
        return hi, mid, x - hi - mid

    def pieces(x, first):
        lane = lax.broadcasted_iota(jnp.int32, x.shape, x.ndim - 1)
        hi, mid, lo = split3(x)
        return jnp.where(lane == first, hi, jnp.where(lane == first + 1, mid,
                                                      jnp.where(lane == first + 2, lo, 0.0)))

    @pl.when(b == 0)
    def _():
        r = lax.broadcasted_iota(jnp.int32, (tq, tq), 0)
        u = lax.broadcasted_iota(jnp.int32, (tq, tq), 1)
        bias_ref[...] = -slope2 * jnp.abs(r - u).astype(F32)
        lane = lax.broadcasted_iota(jnp.int32, (tq, LANES), 1)
        rows = lax.broadcasted_iota(jnp.int32, (tq, LANES), 0).astype(F32)
        ones_q = jnp.where((lane >= 3) & (lane < 9), 1.0, 0.0)
        lane_k = lax.broadcasted_iota(jnp.int32, (tk, LANES), 1)
        cols = lax.broadcasted_iota(jnp.int32, (tk, LANES), 0).astype(F32)
        ones_k = jnp.where(lane_k < 3, 1.0, 0.0)
        for side, sg in enumerate((1.0, -1.0)):
            qf = (pieces(-slope2 * sg * rows, 0) + ones_q).astype(BF16)
            q2_ref[side, 0:tq, LANES:] = qf
            q2_ref[side, tq:2 * tq, LANES:] = qf
            kfeat_ref[side] = (pieces(slope2 * sg * cols, 3) + ones_k).astype(BF16)

    lam = (jnp.exp(jnp.sum(lq1_ref[...] * lk1_ref[...], axis=-1, keepdims=True))
           - jnp.exp(jnp.sum(lq2_ref[...] * lk2_ref[...], axis=-1, keepdims=True)) + lam_init)

    def softmax(slot):
        tiles = [s_ref[slot, :, :, j * LANES:(j + 1) * LANES] for j in range(tk // LANES)]
        m_old = m_ref[...]
        m_new = jnp.maximum(m_old, jnp.max(functools.reduce(jnp.maximum, tiles), axis=-1, keepdims=True))
        alpha = jnp.exp2(m_old - m_new)
        l_new = alpha * l_ref[...]
        for j, s in enumerate(tiles):
            p = jnp.exp2(s - m_new)
            l_new = l_new + p
            p_ref[slot, :, :, j * LANES:(j + 1) * LANES] = p.astype(BF16)
        l_ref[...] = l_new
        m_ref[...] = m_new
        alpha_ref[slot] = alpha

    def q_block(qi):
        q = q_ref[0, qi * tq:(qi + 1) * tq, :]
        lane = lax.broadcasted_iota(jnp.int32, q.shape, 1)
        zero = jnp.zeros_like(q)
        for side in range(2):
            q2_ref[side, 0:tq, :LANES] = jnp.where(lane < HEAD_DK, q, zero)
            q2_ref[side, tq:2 * tq, :LANES] = jnp.where(lane >= HEAD_DK, q, zero)

        m_ref[...] = jnp.full(m_ref.shape, -jnp.inf, F32)
        l_ref[...] = jnp.zeros(l_ref.shape, F32)
        acc_ref[...] = jnp.zeros(acc_ref.shape, F32)

        def chunk_of(i):
            o = i - n_diag
            return jnp.where(i < n_diag, n_diag * qi + i, o + jnp.where(o >= n_diag * qi, n_diag, 0))

        def scores_diag(j, slot):
            ks = (n_diag * qi + j) * tk
            s = lax.dot_general(q2_ref[0, :, :LANES], k_ref[0, ks:ks + tk, :], (((1,), (1,)), ((), ())),
                                preferred_element_type=F32)
            s_ref[slot] = s.reshape(2, tq, tk) + bias_ref[:, j * tk:(j + 1) * tk][None]

        def scores(i, slot):
            c = chunk_of(i)
            ks = pl.multiple_of(c * tk, tk)
            side = (c >= n_diag * qi).astype(jnp.int32)
            sg = 1.0 - 2.0 * side.astype(F32)
            delta = (qi * tq - c * tk).astype(F32)
            dvec = jnp.full((1, LANES), -slope2 * sg * delta, F32)
            kf = kfeat_ref[side] + pieces(dvec, 6).astype(BF16)
            rhs = jnp.concatenate([k_ref[0, pl.ds(ks, tk), :], kf], axis=1)
            s = lax.dot_general(q2_ref[side], rhs, (((1,), (1,)), ((), ())), preferred_element_type=F32)
            s_ref[slot] = s.reshape(2, tq, tk)

        def values(i, slot):
            ks = pl.multiple_of(chunk_of(i) * tk, tk)
            pv = jnp.dot(p_ref[slot].reshape(2 * tq, tk), v_ref[0, pl.ds(ks, tk), :],
                         preferred_element_type=F32)
            acc_ref[...] = alpha_ref[slot] * acc_ref[...] + pv.reshape(2, tq, HEAD_DV)

        def step(i, slot):
            values(i - 1, 1 - slot)
            softmax(slot)
            scores(i + 1, 1 - slot)

        scores_diag(0, 0)
        scores_diag(1, 1)
        softmax(0)

        if n_kc > 2:
            def two_steps(n, _):
                step(2 * n + 1, 1)
                step(2 * n + 2, 0)
                return 0

            n_pairs = (n_kc - 2) // 2
            lax.fori_loop(0, jnp.minimum(h + n_pairs, n_pairs), two_steps, 0)
        values(n_kc - 2, n_kc % 2)
        softmax((n_kc - 1) % 2)
        values(n_kc - 1, (n_kc - 1) % 2)

        l = jnp.sum(l_ref[...], axis=-1, keepdims=True)
        o = acc_ref[0] / l[0] - lam * (acc_ref[1] / l[1])
        o_ref[0, qi * tq:(qi + 1) * tq, :] = (_rmsnorm(o, g_ref[...]) * (1.0 - lam_init)).astype(BF16)

    for qi in range(seq // tq):
        q_block(qi)


def _attention(q, k, v, lq1, lk1, lq2, lk2, subln_g, *, lam_init, tq, tk):
    batch, seq, _ = q.shape
    slopes = jnp.asarray([2.0 ** (-8.0 * (h + 1) / N_HEADS) for h in range(N_HEADS)], F32)
    grid = (N_HEADS, batch)
    vec = lambda a: a.reshape(1, -1).astype(F32)
    small = lambda n: pl.BlockSpec((1, n), lambda h, b: (0, 0))
    head = pl.BlockSpec((1, seq, HEAD_DV), lambda h, b: (b, 0, h))
    return pl.pallas_call(
        functools.partial(_attn_kernel, seq=seq, tq=tq, tk=tk, lam_init=lam_init),
        grid=grid,
        in_specs=[
            pl.BlockSpec(memory_space=pltpu.SMEM),
            head, head, head,
            small(HEAD_DK), small(HEAD_DK), small(HEAD_DK), small(HEAD_DK), small(HEAD_DV),
        ],
        out_specs=head,
        out_shape=jax.ShapeDtypeStruct((batch, seq, N_HEADS * HEAD_DV), BF16),
        scratch_shapes=[
            pltpu.VMEM((tq, tq), F32),
            pltpu.VMEM((2, 2 * tq, 2 * HEAD_DV), BF16),
            pltpu.VMEM((2, tk, LANES), BF16),
            pltpu.VMEM((2, 2, tq, tk), F32),
            pltpu.VMEM((2, 2, tq, tk), BF16),
            pltpu.VMEM((2, 2, tq, LANES), F32),
            pltpu.VMEM((2, tq, LANES), F32),
            pltpu.VMEM((2, tq, LANES), F32),
            pltpu.VMEM((2, tq, HEAD_DV), F32),
        ],
        compiler_params=pltpu.CompilerParams(
            dimension_semantics=("arbitrary", "arbitrary"),
            vmem_limit_bytes=VMEM_LIMIT),
        name="diffattn",
    )(slopes, q, k, v, vec(lq1), vec(lk1), vec(lq2), vec(lk2), vec(subln_g))


def _sigmoid(x):
    return 1.0 / (1.0 + jnp.exp(-x))


def _softplus(x):
    return jnp.maximum(x, 0.0) + jnp.log1p(jnp.exp(-jnp.abs(x)))


def _lru_kernel(xr_ref, gate_ref, cw_ref, cb_ref, wg_ref, bg_ref, lam_ref, y_ref,
                xpad_ref, a_ref, b_ref, h_ref, *, seq, tc, n_grp):
    pad = SUBLANES

    xpad_ref[0:pad, :] = jnp.zeros((pad, LANES), F32)
    xpad_ref[pad + seq:pad + seq + pad, :] = jnp.zeros((pad, LANES), F32)
    xpad_ref[pad:pad + seq, :] = xr_ref[0]

    cw = cw_ref[...]
    cb = cb_ref[...]
    kk = _softplus(-lam_ref[0]) * (-0.5 * RG_C * LOG2E)

    def gates(c, _):
        t0 = pl.multiple_of(c * tc, tc)
        xc = cb
        for j in range(CONV_W):
            xc = xc + xpad_ref[pl.ds(t0 + pad - 2 + j, tc), :] * cw[j:j + 1, :]
        th = jnp.tanh(jnp.dot(xc.astype(BF16), wg_ref[0], preferred_element_type=F32) + bg_ref[0])
        xh = 0.5 * xc
        for d in range(2):
            t_r = th[:, (2 * d) * LANES:(2 * d + 1) * LANES]
            t_i = th[:, (2 * d + 1) * LANES:(2 * d + 2) * LANES]
            kd = kk[d:d + 1, :]
            a = jnp.exp2(kd * t_r + kd)
            m2 = jnp.maximum(1.0 - a * a, 1e-12)
            b = (m2 * lax.rsqrt(m2)) * (t_i * xh + xh)
            a_ref[d, pl.ds(t0, tc), :] = a
            b_ref[d, pl.ds(t0, tc), :] = b
        return 0

    part = seq // SUBLANES + 4
    assert (part // 4) % 2 == 1 and part % n_grp == 0
    seg = part // n_grp
    n_pad = SUBLANES * part - seq
    for d in range(2):
        a_ref[d, seq:seq + n_pad, :] = jnp.ones((n_pad, LANES), F32)
        b_ref[d, seq:seq + n_pad, :] = jnp.zeros((n_pad, LANES), F32)

    def tile(g, t):
        return pl.ds(g * seg + t, SUBLANES, stride=part)

    lax.fori_loop(0, seq // tc, gates, 0, unroll=2)

    zeros = jnp.zeros((SUBLANES, LANES), F32)
    ones = jnp.ones((SUBLANES, LANES), F32)

    def pass1(n, carry):
        out = []
        for d in range(2):
            t = n if d == 0 else seg - 1 - n
            hs, ps = carry[d]
            nh, npr = [], []
            for g in range(n_grp):
                a = a_ref[d, tile(g, t), :]
                nh.append(a * hs[g] + b_ref[d, tile(g, t), :])
                npr.append(a * ps[g])
            out.append((tuple(nh), tuple(npr)))
        return tuple(out)

    ends = lax.fori_loop(0, seg, pass1,
                         tuple(((zeros,) * n_grp, (ones,) * n_grp) for _ in range(2)), unroll=2)

    def chain_rows(e, q, reverse):
        row = lax.broadcasted_iota(jnp.int32, (SUBLANES, LANES), 0)
        edge = (SUBLANES - 1) if reverse else 0
        shift = (SUBLANES - 1) if reverse else 1
        carry = zeros
        for _ in range(SUBLANES - 1):
            carry = jnp.where(row == edge, 0.0, pltpu.roll(e + q * carry, shift, 0))
        return carry

    def entry_states(h_end, p_end, reverse):
        order = range(n_grp - 1, -1, -1) if reverse else range(n_grp)
        local, prod = {}, {}
        c, q = zeros, ones
        for g in order:
            local[g], prod[g] = c, q
            c = h_end[g] + p_end[g] * c
            q = p_end[g] * q
        row_in = chain_rows(c, q, reverse)
        return tuple(local[g] + prod[g] * row_in for g in range(n_grp))

    def pass2(n, carry):
        out = []
        for d in range(2):
            t = n if d == 0 else seg - 1 - n
            nh = []
            for g in range(n_grp):
                hcur = a_ref[d, tile(g, t), :] * carry[d][g] + b_ref[d, tile(g, t), :]
                h_ref[d, tile(g, t), :] = hcur
                nh.append(hcur)
            out.append(tuple(nh))
        return tuple(out)

    lax.fori_loop(0, seg, pass2,
                  tuple(entry_states(ends[d][0], ends[d][1], d == 1) for d in range(2)), unroll=2)

    c1 = math.sqrt(2.0 / math.pi)

    def finish(c, _):
        t0 = pl.multiple_of(c * tc, tc)
        hsum = h_ref[0, pl.ds(t0, tc), :] + h_ref[1, pl.ds(t0, tc), :]
        x = gate_ref[0, pl.ds(t0, tc), :]
        th = jnp.tanh(x * ((x * x) * (c1 * 0.044715) + c1))
        xh = 0.5 * x
        y_ref[0, pl.ds(t0, tc), :] = (hsum * (xh * th + xh)).astype(BF16)
        return 0

    lax.fori_loop(0, seq // tc, finish, 0)


def _lru_gate_weights(w_rg, b_rg, w_ig, b_ig):
    blk = w_rg.shape[-1]
    per = LANES // blk
    n_groups = N_LRU_BLOCKS // per

    def dense(w):
        w = w.reshape(n_groups, per, blk, blk)
        eye = jnp.eye(per, dtype=w.dtype)
        return jnp.einsum('gpcd,pq->gpcqd', w, eye).reshape(n_groups, LANES, LANES)

    wg = jnp.concatenate([dense(w_rg[0]), dense(w_ig[0]), dense(w_rg[1]), dense(w_ig[1])], axis=-1)
    bias = lambda v: v.reshape(n_groups, 1, LANES)
    bg = jnp.concatenate([bias(b_rg[0]), bias(b_ig[0]), bias(b_rg[1]), bias(b_ig[1])], axis=-1)
    return (0.5 * wg).astype(BF16), (0.5 * bg).astype(F32)


def _rglru(xr, gate, conv_w, conv_b, wg, bg, lru_lambda, *, tc, n_grp):
    batch, seq, lru_w = xr.shape
    n_groups = lru_w // LANES
    lam = lru_lambda.reshape(2, n_groups, LANES).transpose(1, 0, 2)
    grid = (batch, n_groups)
    tok = pl.BlockSpec((1, seq, LANES), lambda b, j: (b, 0, j))
    return pl.pallas_call(
        functools.partial(_lru_kernel, seq=seq, tc=tc, n_grp=n_grp),
        grid=grid,
        in_specs=[
            tok, tok,
            pl.BlockSpec((CONV_W, LANES), lambda b, j: (0, j)),
            pl.BlockSpec((1, LANES), lambda b, j: (0, j)),
            pl.BlockSpec((1, LANES, 4 * LANES), lambda b, j: (j, 0, 0)),
            pl.BlockSpec((1, 1, 4 * LANES), lambda b, j: (j, 0, 0)),
            pl.BlockSpec((1, 2, LANES), lambda b, j: (j, 0, 0)),
        ],
        out_specs=tok,
        out_shape=jax.ShapeDtypeStruct((batch, seq, lru_w), BF16),
        scratch_shapes=[
            pltpu.VMEM((seq + 2 * SUBLANES, LANES), F32),
            pltpu.VMEM((2, seq + 4 * SUBLANES, LANES), F32),
            pltpu.VMEM((2, seq + 4 * SUBLANES, LANES), F32),
            pltpu.VMEM((2, seq + 4 * SUBLANES, LANES), F32),
        ],
        compiler_params=pltpu.CompilerParams(
            dimension_semantics=("arbitrary", "arbitrary"), vmem_limit_bytes=VMEM_LIMIT),
        name="rglru",
    )(xr, gate, conv_w, conv_b.reshape(1, lru_w), wg, bg, lam)


def _ffn_kernel(x_ref, o_ref, y_ref, wo_ref, g2_ref, wg_ref, wu_ref, wd_ref, g3_ref, out_ref, *, att_w):
    x1 = (x_ref[...]
          + jnp.dot(o_ref[...], wo_ref[0:att_w, :], preferred_element_type=F32)
          + jnp.dot(y_ref[...], wo_ref[att_w:, :], preferred_element_type=F32))
    h2 = _rmsnorm(x1, g2_ref[...]).astype(BF16)
    gt = jnp.dot(h2, wg_ref[...], preferred_element_type=F32)
    up = jnp.dot(h2, wu_ref[...], preferred_element_type=F32)
    act = ((gt * _sigmoid(gt)) * up).astype(BF16)
    x2 = x1 + jnp.dot(act, wd_ref[...], preferred_element_type=F32)
    out_ref[...] = _rmsnorm(x2, g3_ref[...])


def _out_ffn(x2d, o2d, y2d, w_out, norm_ffn, w_gate, w_up, w_down, norm_final, *, tm):
    n_tok, d_model = x2d.shape
    att_w = o2d.shape[1]
    lru_w = y2d.shape[1]
    row = lambda i: (i, 0)
    const = lambda i: (0, 0)
    resident = lambda a: pl.BlockSpec(a.shape, const, pipeline_mode=pl.Buffered(1))
    g2 = norm_ffn.reshape(1, d_model)
    g3 = norm_final.reshape(1, d_model)
    return pl.pallas_call(
        functools.partial(_ffn_kernel, att_w=att_w),
        grid=(n_tok // tm,),
        in_specs=[
            pl.BlockSpec((tm, d_model), row),
            pl.BlockSpec((tm, att_w), row),
            pl.BlockSpec((tm, lru_w), row),
            resident(w_out), resident(g2), resident(w_gate), resident(w_up), resident(w_down),
            resident(g3),
        ],
        out_specs=pl.BlockSpec((tm, d_model), row),
        out_shape=jax.ShapeDtypeStruct((n_tok, d_model), F32),
        compiler_params=pltpu.CompilerParams(
            dimension_semantics=("arbitrary",), vmem_limit_bytes=VMEM_LIMIT),
        name="out_ffn",
    )(x2d, o2d, y2d, w_out, g2, w_gate, w_up, w_down, g3)


def _trunk(x, p, *, tm_in, tq, tk, tc, n_grp, tm_ffn):
    batch, seq, d_model = x.shape
    x2d = x.reshape(batch * seq, d_model)
    depth = p["w_in"].shape[0]
    for l in range(depth):
        last = l == depth - 1
        lam_init = 0.8 - 0.6 * math.exp(-0.3 * l)
        q, k, v, xr, gate = _inproj(x2d, p["norm_mix"][l], p["w_in"][l].astype(BF16), tm=tm_in)
        att_w = v.shape[1]
        lru_w = xr.shape[1]
        shp = lambda a: a.reshape(batch, seq, a.shape[1])
        o = _attention(shp(q), shp(k), shp(v), p["lambda_q1"][l], p["lambda_k1"][l],
                       p["lambda_q2"][l], p["lambda_k2"][l], p["subln_g"][l],
                       lam_init=lam_init, tq=tq, tk=tk)
        wg, bg = _lru_gate_weights(p["w_rg"][l], p["b_rg"][l], p["w_ig"][l], p["b_ig"][l])
        y = _rglru(shp(xr), shp(gate), p["conv_w"][l], p["conv_b"][l], wg, bg, p["lru_lambda"][l],
                   tc=tc, n_grp=n_grp)
        assert last, "only the last layer's fused final norm is implemented"
        x2d = _out_ffn(x2d, o.reshape(batch * seq, att_w), y.reshape(batch * seq, lru_w),
                       p["w_out"][l].astype(BF16), p["norm_ffn"][l], p["w_gate"][l].astype(BF16),
                       p["w_up"][l].astype(BF16), p["w_down"][l].astype(BF16), p["norm_final"],
                       tm=tm_ffn)
    return x2d.reshape(batch, seq, d_model)


def _tiles(seq):
    return dict(tm_in=1024, tq=min(1024, seq), tk=min(512, seq), tc=min(256, seq // SUBLANES), n_grp=4,
                tm_ffn=512)


def kernel(x_prompt, x_sample, norm_mix, w_in, conv_w, conv_b, w_rg, b_rg, w_ig, b_ig, lru_lambda, lambda_q1, lambda_k1, lambda_q2, lambda_k2, subln_g, w_out, norm_ffn, w_gate, w_up, w_down, norm_final):
    p = dict(norm_mix=norm_mix, w_in=w_in, conv_w=conv_w, conv_b=conv_b, w_rg=w_rg, b_rg=b_rg,
             w_ig=w_ig, b_ig=b_ig, lru_lambda=lru_lambda, lambda_q1=lambda_q1, lambda_k1=lambda_k1,
             lambda_q2=lambda_q2, lambda_k2=lambda_k2, subln_g=subln_g, w_out=w_out,
             norm_ffn=norm_ffn, w_gate=w_gate, w_up=w_up, w_down=w_down, norm_final=norm_final)
    y_prompt = _trunk(x_prompt, p, **_tiles(x_prompt.shape[1]))
    y_sample = _trunk(x_sample, p, **_tiles(x_sample.shape[1]))
    return (y_prompt, y_sample)
```

```python
import functools
import math

import jax
import jax.numpy as jnp
from jax import lax
from jax.experimental import pallas as pl
from jax.experimental.pallas import tpu as pltpu

F32 = jnp.float32
BF16 = jnp.bfloat16

N_HEADS = 4
HEAD_DV = 128
HEAD_DK = 64
N_LRU_BLOCKS = 8
CONV_W = 4
RG_C = 8.0
NORM_EPS = 1e-6
LOG2E = math.log2(math.e)
LANES = 128
SUBLANES = 8
VMEM_LIMIT = 56 * 1024 * 1024


def _rmsnorm(x, g):
    ms = jnp.mean(x * x, axis=-1, keepdims=True)
    return (x * lax.rsqrt(ms + NORM_EPS)) * g


def _inproj_kernel(x_ref, g_ref, w_ref, q_ref, k_ref, v_ref, xr_ref, gate_ref, *, qk_w, att_w, lru_w):
    h = _rmsnorm(x_ref[...], g_ref[...]).astype(BF16)

    def proj(lo, width):
        return jnp.dot(h, w_ref[:, lo:lo + width], preferred_element_type=F32)

    q_ref[...] = (proj(0, qk_w) * (LOG2E / math.sqrt(HEAD_DK))).astype(BF16)
    k_ref[...] = proj(qk_w, qk_w).astype(BF16)
    v_ref[...] = proj(2 * qk_w, att_w).astype(BF16)
    xr_ref[...] = proj(2 * qk_w + att_w, lru_w)
    gate_ref[...] = proj(2 * qk_w + att_w + lru_w, lru_w)


def _inproj(x2d, norm_g, w_in_bf16, *, tm):
    n_tok, d_model = x2d.shape
    qk_w = N_HEADS * 2 * HEAD_DK
    att_w = N_HEADS * HEAD_DV
    lru_w = (w_in_bf16.shape[1] - 2 * qk_w - att_w) // 2
    grid = (n_tok // tm,)
    row = lambda i: (i, 0)
    const = lambda i: (0, 0)
    return pl.pallas_call(
        functools.partial(_inproj_kernel, qk_w=qk_w, att_w=att_w, lru_w=lru_w),
        grid=grid,
        in_specs=[
            pl.BlockSpec((tm, d_model), row),
            pl.BlockSpec((1, d_model), const),
            pl.BlockSpec(w_in_bf16.shape, const, pipeline_mode=pl.Buffered(1)),
        ],
        out_specs=[
            pl.BlockSpec((tm, qk_w), row),
            pl.BlockSpec((tm, qk_w), row),
            pl.BlockSpec((tm, att_w), row),
            pl.BlockSpec((tm, lru_w), row),
            pl.BlockSpec((tm, lru_w), row),
        ],
        out_shape=[
            jax.ShapeDtypeStruct((n_tok, qk_w), BF16),
            jax.ShapeDtypeStruct((n_tok, qk_w), BF16),
            jax.ShapeDtypeStruct((n_tok, att_w), BF16),
            jax.ShapeDtypeStruct((n_tok, lru_w), F32),
            jax.ShapeDtypeStruct((n_tok, lru_w), F32),
        ],
        compiler_params=pltpu.CompilerParams(
            dimension_semantics=("arbitrary",), vmem_limit_bytes=VMEM_LIMIT),
        name="inproj",
    )(x2d, norm_g.reshape(1, d_model), w_in_bf16)


def _attn_kernel(slopes_ref, q_ref, k_ref, v_ref, lq1_ref, lk1_ref, lq2_ref, lk2_ref, g_ref,
                 o_ref, bias_ref, q2_ref, kfeat_ref, s_ref, mx_ref, p_ref, alpha_ref, m_ref, l_ref, acc_ref,
                 *, seq, tq, tk, lam_init):
    h = pl.program_id(0)
    b = pl.program_id(1)
    n_kc = seq // tk
    n_diag = tq // tk
    assert n_diag == 2 and n_kc >= n_diag and n_kc % 2 == 0
    slope2 = LOG2E * slopes_ref[h]

    def split3(x):
        hi = x.astype(BF16).astype(F32)
        mid = (x - hi).astype(BF16).astype(F32)
        return hi, mid, x - hi - mid

    def pieces(x, first):
        lane = lax.broadcasted_iota(jnp.int32, x.shape, x.ndim - 1)
        hi, mid, lo = split3(x)
        return jnp.where(lane == first, hi, jnp.where(lane == first + 1, mid,
                                                      jnp.where(lane == first + 2, lo, 0.0)))

    @pl.when(b == 0)
    def _():
        r = lax.broadcasted_iota(jnp.int32, (tq, tq), 0)
        u = lax.broadcasted_iota(jnp.int32, (tq, tq), 1)
        bias_ref[...] = -slope2 * jnp.abs(r - u).astype(F32)
        lane = lax.broadcasted_iota(jnp.int32, (tq, LANES), 1)
        rows = lax.broadcasted_iota(jnp.int32, (tq, LANES), 0).astype(F32)
        ones_q = jnp.where((lane >= 3) & (lane < 9), 1.0, 0.0)
        lane_k = lax.broadcasted_iota(jnp.int32, (tk, LANES), 1)
        cols = lax.broadcasted_iota(jnp.int32, (tk, LANES), 0).astype(F32)
        ones_k = jnp.where(lane_k < 3, 1.0, 0.0)
        for side, sg in enumerate((1.0, -1.0)):
            qf = (pieces(-slope2 * sg * rows, 0) + ones_q).astype(BF16)
            q2_ref[side, 0:tq, LANES:] = qf
            q2_ref[side, tq:2 * tq, LANES:] = qf
            kfeat_ref[side] = (pieces(slope2 * sg * cols, 3) + ones_k).astype(BF16)

    lam = (jnp.exp(jnp.sum(lq1_ref[...] * lk1_ref[...], axis=-1, keepdims=True))
           - jnp.exp(jnp.sum(lq2_ref[...] * lk2_ref[...], axis=-1, keepdims=True)) + lam_init)

    def put_scores(slot, s):
        tiles = [s[:, :, j * LANES:(j + 1) * LANES] for j in range(tk // LANES)]
        mx_ref[slot] = functools.reduce(jnp.maximum, tiles)
        s_ref[slot] = s

    def softmax(slot):
        tiles = [s_ref[slot, :, :, j * LANES:(j + 1) * LANES] for j in range(tk // LANES)]
        m_old = m_ref[...]
        m_new = jnp.maximum(m_old, jnp.max(mx_ref[slot], axis=-1, keepdims=True))
        alpha = jnp.exp2(m_old - m_new)
        l_new = alpha * l_ref[...]
        for j, s in enumerate(tiles):
            p = jnp.exp2(s - m_new)
            l_new = l_new + p
            p_ref[slot, :, :, j * LANES:(j + 1) * LANES] = p.astype(BF16)
        l_ref[...] = l_new
        m_ref[...] = m_new
        alpha_ref[slot] = alpha

    def q_block(qi):
        q = q_ref[0, qi * tq:(qi + 1) * tq, :]
        lane = lax.broadcasted_iota(jnp.int32, q.shape, 1)
        zero = jnp.zeros_like(q)
        for side in range(2):
            q2_ref[side, 0:tq, :LANES] = jnp.where(lane < HEAD_DK, q, zero)
            q2_ref[side, tq:2 * tq, :LANES] = jnp.where(lane >= HEAD_DK, q, zero)

        m_ref[...] = jnp.full(m_ref.shape, -jnp.inf, F32)
        l_ref[...] = jnp.zeros(l_ref.shape, F32)
        acc_ref[...] = jnp.zeros(acc_ref.shape, F32)

        def chunk_of(i):
            o = i - n_diag
            return jnp.where(i < n_diag, n_diag * qi + i, o + jnp.where(o >= n_diag * qi, n_diag, 0))

        def scores_diag(j, slot):
            ks = (n_diag * qi + j) * tk
            s = lax.dot_general(q2_ref[0, :, :LANES], k_ref[0, ks:ks + tk, :], (((1,), (1,)), ((), ())),
                                preferred_element_type=F32)
            put_scores(slot, s.reshape(2, tq, tk) + bias_ref[:, j * tk:(j + 1) * tk][None])

        def scores(i, slot):
            c = chunk_of(i)
            ks = pl.multiple_of(c * tk, tk)
            side = (c >= n_diag * qi).astype(jnp.int32)
            sg = 1.0 - 2.0 * side.astype(F32)
            delta = (qi * tq - c * tk).astype(F32)
            dvec = jnp.full((1, LANES), -slope2 * sg * delta, F32)
            kf = kfeat_ref[side] + pieces(dvec, 6).astype(BF16)
            rhs = jnp.concatenate([k_ref[0, pl.ds(ks, tk), :], kf], axis=1)
            s = lax.dot_general(q2_ref[side], rhs, (((1,), (1,)), ((), ())), preferred_element_type=F32)
            put_scores(slot, s.reshape(2, tq, tk))

        def values(i, slot):
            ks = pl.multiple_of(chunk_of(i) * tk, tk)
            pv = jnp.dot(p_ref[slot].reshape(2 * tq, tk), v_ref[0, pl.ds(ks, tk), :],
                         preferred_element_type=F32)
            acc_ref[...] = alpha_ref[slot] * acc_ref[...] + pv.reshape(2, tq, HEAD_DV)

        def step(i, slot):
            values(i - 1, 1 - slot)
            softmax(slot)
            scores(i + 1, 1 - slot)

        scores_diag(0, 0)
        scores_diag(1, 1)
        softmax(0)

        if n_kc > 2:
            def two_steps(n, _):
                step(2 * n + 1, 1)
                step(2 * n + 2, 0)
                return 0

            n_pairs = (n_kc - 2) // 2
            lax.fori_loop(0, jnp.minimum(h + n_pairs, n_pairs), two_steps, 0)
        values(n_kc - 2, n_kc % 2)
        softmax((n_kc - 1) % 2)
        values(n_kc - 1, (n_kc - 1) % 2)

        l = jnp.sum(l_ref[...], axis=-1, keepdims=True)
        o = acc_ref[0] / l[0] - lam * (acc_ref[1] / l[1])
        o_ref[0, qi * tq:(qi + 1) * tq, :] = (_rmsnorm(o, g_ref[...]) * (1.0 - lam_init)).astype(BF16)

    for qi in range(seq // tq):
        q_block(qi)


def _attention(q, k, v, lq1, lk1, lq2, lk2, subln_g, *, lam_init, tq, tk):
    batch, seq, _ = q.shape
    slopes = jnp.asarray([2.0 ** (-8.0 * (h + 1) / N_HEADS) for h in range(N_HEADS)], F32)
    grid = (N_HEADS, batch)
    vec = lambda a: a.reshape(1, -1).astype(F32)
    small = lambda n: pl.BlockSpec((1, n), lambda h, b: (0, 0))
    head = pl.BlockSpec((1, seq, HEAD_DV), lambda h, b: (b, 0, h))
    return pl.pallas_call(
        functools.partial(_attn_kernel, seq=seq, tq=tq, tk=tk, lam_init=lam_init),
        grid=grid,
        in_specs=[
            pl.BlockSpec(memory_space=pltpu.SMEM),
            head, head, head,
            small(HEAD_DK), small(HEAD_DK), small(HEAD_DK), small(HEAD_DK), small(HEAD_DV),
        ],
        out_specs=head,
        out_shape=jax.ShapeDtypeStruct((batch, seq, N_HEADS * HEAD_DV), BF16),
        scratch_shapes=[
            pltpu.VMEM((tq, tq), F32),
            pltpu.VMEM((2, 2 * tq, 2 * HEAD_DV), BF16),
            pltpu.VMEM((2, tk, LANES), BF16),
            pltpu.VMEM((2, 2, tq, tk), F32),
            pltpu.VMEM((2, 2, tq, LANES), F32),
            pltpu.VMEM((2, 2, tq, tk), BF16),
            pltpu.VMEM((2, 2, tq, LANES), F32),
            pltpu.VMEM((2, tq, LANES), F32),
            pltpu.VMEM((2, tq, LANES), F32),
            pltpu.VMEM((2, tq, HEAD_DV), F32),
        ],
        compiler_params=pltpu.CompilerParams(
            dimension_semantics=("arbitrary", "arbitrary"),
            vmem_limit_bytes=VMEM_LIMIT),
        name="diffattn",
    )(slopes, q, k, v, vec(lq1), vec(lk1), vec(lq2), vec(lk2), vec(subln_g))


def _sigmoid(x):
    return 1.0 / (1.0 + jnp.exp(-x))


def _softplus(x):
    return jnp.maximum(x, 0.0) + jnp.log1p(jnp.exp(-jnp.abs(x)))


def _lru_kernel(xr_ref, gate_ref, cw_ref, cb_ref, wg_ref, bg_ref, lam_ref, y_ref,
                xpad_ref, a_ref, b_ref, h_ref, *, seq, tc, n_grp):
    pad = SUBLANES

    xpad_ref[0:pad, :] = jnp.zeros((pad, LANES), F32)
    xpad_ref[pad + seq:pad + seq + pad, :] = jnp.zeros((pad, LANES), F32)
    xpad_ref[pad:pad + seq, :] = xr_ref[0]

    cw = cw_ref[...]
    cb = cb_ref[...]
    kk = _softplus(-lam_ref[0]) * (-0.5 * RG_C * LOG2E)

    def gates(c, _):
        t0 = pl.multiple_of(c * tc, tc)
        xc = cb
        for j in range(CONV_W):
            xc = xc + xpad_ref[pl.ds(t0 + pad - 2 + j, tc), :] * cw[j:j + 1, :]
        th = jnp.tanh(jnp.dot(xc.astype(BF16), wg_ref[0], preferred_element_type=F32) + bg_ref[0])
        xh = 0.5 * xc
        for d in range(2):
            t_r = th[:, (2 * d) * LANES:(2 * d + 1) * LANES]
            t_i = th[:, (2 * d + 1) * LANES:(2 * d + 2) * LANES]
            kd = kk[d:d + 1, :]
            a = jnp.exp2(kd * t_r + kd)
            m2 = jnp.maximum(1.0 - a * a, 1e-12)
            b = (m2 * lax.rsqrt(m2)) * (t_i * xh + xh)
            a_ref[d, pl.ds(t0, tc), :] = a
            b_ref[d, pl.ds(t0, tc), :] = b
        return 0

    part = seq // SUBLANES + 4
    assert (part // 4) % 2 == 1 and part % n_grp == 0
    seg = part // n_grp
    n_pad = SUBLANES * part - seq
    for d in range(2):
        a_ref[d, seq:seq + n_pad, :] = jnp.ones((n_pad, LANES), F32)
        b_ref[d, seq:seq + n_pad, :] = jnp.zeros((n_pad, LANES), F32)

    def tile(g, t):
        return pl.ds(g * seg + t, SUBLANES, stride=part)

    lax.fori_loop(0, seq // tc, gates, 0, unroll=2)

    zeros = jnp.zeros((SUBLANES, LANES), F32)
    ones = jnp.ones((SUBLANES, LANES), F32)

    def pass1(n, carry):
        out = []
        for d in range(2):
            t = n if d == 0 else seg - 1 - n
            hs, ps = carry[d]
            nh, npr = [], []
            for g in range(n_grp):
                a = a_ref[d, tile(g, t), :]
                nh.append(a * hs[g] + b_ref[d, tile(g, t), :])
                npr.append(a * ps[g])
            out.append((tuple(nh), tuple(npr)))
        return tuple(out)

    ends = lax.fori_loop(0, seg, pass1,
                         tuple(((zeros,) * n_grp, (ones,) * n_grp) for _ in range(2)), unroll=2)

    def chain_rows(e, q, reverse):
        row = lax.broadcasted_iota(jnp.int32, (SUBLANES, LANES), 0)
        edge = (SUBLANES - 1) if reverse else 0
        shift = (SUBLANES - 1) if reverse else 1
        carry = zeros
        for _ in range(SUBLANES - 1):
            carry = jnp.where(row == edge, 0.0, pltpu.roll(e + q * carry, shift, 0))
        return carry

    def entry_states(h_end, p_end, reverse):
        order = range(n_grp - 1, -1, -1) if reverse else range(n_grp)
        local, prod = {}, {}
        c, q = zeros, ones
        for g in order:
            local[g], prod[g] = c, q
            c = h_end[g] + p_end[g] * c
            q = p_end[g] * q
        row_in = chain_rows(c, q, reverse)
        return tuple(local[g] + prod[g] * row_in for g in range(n_grp))

    def pass2(n, carry):
        out = []
        for d in range(2):
            t = n if d == 0 else seg - 1 - n
            nh = []
            for g in range(n_grp):
                hcur = a_ref[d, tile(g, t), :] * carry[d][g] + b_ref[d, tile(g, t), :]
                h_ref[d, tile(g, t), :] = hcur
                nh.append(hcur)
            out.append(tuple(nh))
        return tuple(out)

    lax.fori_loop(0, seg, pass2,
                  tuple(entry_states(ends[d][0], ends[d][1], d == 1) for d in range(2)), unroll=2)

    c1 = math.sqrt(2.0 / math.pi)

    def finish(c, _):
        t0 = pl.multiple_of(c * tc, tc)
        hsum = h_ref[0, pl.ds(t0, tc), :] + h_ref[1, pl.ds(t0, tc), :]
        x = gate_ref[0, pl.ds(t0, tc), :]
        th = jnp.tanh(x * ((x * x) * (c1 * 0.044715) + c1))
        xh = 0.5 * x
        y_ref[0, pl.ds(t0, tc), :] = (hsum * (xh * th + xh)).astype(BF16)
        return 0

    lax.fori_loop(0, seq // tc, finish, 0)


def _lru_gate_weights(w_rg, b_rg, w_ig, b_ig):
    blk = w_rg.shape[-1]
    per = LANES // blk
    n_groups = N_LRU_BLOCKS // per

    def dense(w):
        w = w.reshape(n_groups, per, blk, blk)
        eye = jnp.eye(per, dtype=w.dtype)
        return jnp.einsum('gpcd,pq->gpcqd', w, eye).reshape(n_groups, LANES, LANES)

    wg = jnp.concatenate([dense(w_rg[0]), dense(w_ig[0]), dense(w_rg[1]), dense(w_ig[1])], axis=-1)
    bias = lambda v: v.reshape(n_groups, 1, LANES)
    bg = jnp.concatenate([bias(b_rg[0]), bias(b_ig[0]), bias(b_rg[1]), bias(b_ig[1])], axis=-1)
    return (0.5 * wg).astype(BF16), (0.5 * bg).astype(F32)


def _rglru(xr, gate, conv_w, conv_b, wg, bg, lru_lambda, *, tc, n_grp):
    batch, seq, lru_w = xr.shape
    n_groups = lru_w // LANES
    lam = lru_lambda.reshape(2, n_groups, LANES).transpose(1, 0, 2)
    grid = (batch, n_groups)
    tok = pl.BlockSpec((1, seq, LANES), lambda b, j: (b, 0, j))
    return pl.pallas_call(
        functools.partial(_lru_kernel, seq=seq, tc=tc, n_grp=n_grp),
        grid=grid,
        in_specs=[
            tok, tok,
            pl.BlockSpec((CONV_W, LANES), lambda b, j: (0, j)),
            pl.BlockSpec((1, LANES), lambda b, j: (0, j)),
            pl.BlockSpec((1, LANES, 4 * LANES), lambda b, j: (j, 0, 0)),
            pl.BlockSpec((1, 1, 4 * LANES), lambda b, j: (j, 0, 0)),
            pl.BlockSpec((1, 2, LANES), lambda b, j: (j, 0, 0)),
        ],
        out_specs=tok,
        out_shape=jax.ShapeDtypeStruct((batch, seq, lru_w), BF16),
        scratch_shapes=[
            pltpu.VMEM((seq + 2 * SUBLANES, LANES), F32),
            pltpu.VMEM((2, seq + 4 * SUBLANES, LANES), F32),
            pltpu.VMEM((2, seq + 4 * SUBLANES, LANES), F32),
            pltpu.VMEM((2, seq + 4 * SUBLANES, LANES), F32),
        ],
        compiler_params=pltpu.CompilerParams(
            dimension_semantics=("arbitrary", "arbitrary"), vmem_limit_bytes=VMEM_LIMIT),
        name="rglru",
    )(xr, gate, conv_w, conv_b.reshape(1, lru_w), wg, bg, lam)


def _ffn_kernel(x_ref, o_ref, y_ref, wo_ref, g2_ref, wg_ref, wu_ref, wd_ref, g3_ref, out_ref, *, att_w):
    x1 = (x_ref[...]
          + jnp.dot(o_ref[...], wo_ref[0:att_w, :], preferred_element_type=F32)
          + jnp.dot(y_ref[...], wo_ref[att_w:, :], preferred_element_type=F32))
    h2 = _rmsnorm(x1, g2_ref[...]).astype(BF16)
    gt = jnp.dot(h2, wg_ref[...], preferred_element_type=F32)
    up = jnp.dot(h2, wu_ref[...], preferred_element_type=F32)
    act = ((gt * _sigmoid(gt)) * up).astype(BF16)
    x2 = x1 + jnp.dot(act, wd_ref[...], preferred_element_type=F32)
    out_ref[...] = _rmsnorm(x2, g3_ref[...])


def _out_ffn(x2d, o2d, y2d, w_out, norm_ffn, w_gate, w_up, w_down, norm_final, *, tm):
    n_tok, d_model = x2d.shape
    att_w = o2d.shape[1]
    lru_w = y2d.shape[1]
    row = lambda i: (i, 0)
    const = lambda i: (0, 0)
    resident = lambda a: pl.BlockSpec(a.shape, const, pipeline_mode=pl.Buffered(1))
    g2 = norm_ffn.reshape(1, d_model)
    g3 = norm_final.reshape(1, d_model)
    return pl.pallas_call(
        functools.partial(_ffn_kernel, att_w=att_w),
        grid=(n_tok // tm,),
        in_specs=[
            pl.BlockSpec((tm, d_model), row),
            pl.BlockSpec((tm, att_w), row),
            pl.BlockSpec((tm, lru_w), row),
            resident(w_out), resident(g2), resident(w_gate), resident(w_up), resident(w_down),
            resident(g3),
        ],
        out_specs=pl.BlockSpec((tm, d_model), row),
        out_shape=jax.ShapeDtypeStruct((n_tok, d_model), F32),
        compiler_params=pltpu.CompilerParams(
            dimension_semantics=("arbitrary",), vmem_limit_bytes=VMEM_LIMIT),
        name="out_ffn",
    )(x2d, o2d, y2d, w_out, g2, w_gate, w_up, w_down, g3)


def _trunk(x, p, *, tm_in, tq, tk, tc, n_grp, tm_ffn):
    batch, seq, d_model = x.shape
    x2d = x.reshape(batch * seq, d_model)
    depth = p["w_in"].shape[0]
    for l in range(depth):
        last = l == depth - 1
        lam_init = 0.8 - 0.6 * math.exp(-0.3 * l)
        q, k, v, xr, gate = _inproj(x2d, p["norm_mix"][l], p["w_in"][l].astype(BF16), tm=tm_in)
        att_w = v.shape[1]
        lru_w = xr.shape[1]
        shp = lambda a: a.reshape(batch, seq, a.shape[1])
        o = _attention(shp(q), shp(k), shp(v), p["lambda_q1"][l], p["lambda_k1"][l],
                       p["lambda_q2"][l], p["lambda_k2"][l], p["subln_g"][l],
                       lam_init=lam_init, tq=tq, tk=tk)
        wg, bg = _lru_gate_weights(p["w_rg"][l], p["b_rg"][l], p["w_ig"][l], p["b_ig"][l])
        y = _rglru(shp(xr), shp(gate), p["conv_w"][l], p["conv_b"][l], wg, bg, p["lru_lambda"][l],
                   tc=tc, n_grp=n_grp)
        assert last, "only the last layer's fused final norm is implemented"
        x2d = _out_ffn(x2d, o.reshape(batch * seq, att_w), y.reshape(batch * seq, lru_w),
                       p["w_out"][l].astype(BF16), p["norm_ffn"][l], p["w_gate"][l].astype(BF16),
                       p["w_up"][l].astype(BF16), p["w_down"][l].astype(BF16), p["norm_final"],
                       tm=tm_ffn)
    return x2d.reshape(batch, seq, d_model)


def _tiles(seq):
    return dict(tm_in=1024, tq=min(1024, seq), tk=min(512, seq), tc=min(256, seq // SUBLANES), n_grp=4,
                tm_ffn=512)


def kernel(x_prompt, x_sample, norm_mix, w_in, conv_w, conv_b, w_rg, b_rg, w_ig, b_ig, lru_lambda, lambda_q1, lambda_k1, lambda_q2, lambda_k2, subln_g, w_out, norm_ffn, w_gate, w_up, w_down, norm_final):
    p = dict(norm_mix=norm_mix, w_in=w_in, conv_w=conv_w, conv_b=conv_b, w_rg=w_rg, b_rg=b_rg,
             w_ig=w_ig, b_ig=b_ig, lru_lambda=lru_lambda, lambda_q1=lambda_q1, lambda_k1=lambda_k1,
             lambda_q2=lambda_q2, lambda_k2=lambda_k2, subln_g=subln_g, w_out=w_out,
             norm_ffn=norm_ffn, w_gate=w_gate, w_up=w_up, w_down=w_down, norm_final=norm_final)
    y_prompt = _trunk(x_prompt, p, **_tiles(x_prompt.shape[1]))
    y_sample = _trunk(x_sample, p, **_tiles(x_sample.shape[1]))
    return (y_prompt, y_sample)
```

```python
import functools
import math

import jax
import jax.numpy as jnp
from jax import lax
from jax.experimental import pallas as pl
from jax.experimental.pallas import tpu as pltpu

F32 = jnp.float32
BF16 = jnp.bfloat16

N_HEADS = 4
HEAD_DV = 128
HEAD_DK = 64
N_LRU_BLOCKS = 8
CONV_W = 4
RG_C = 8.0
NORM_EPS = 1e-6
LOG2E = math.log2(math.e)
LANES = 128
SUBLANES = 8
VMEM_LIMIT = 56 * 1024 * 1024


def _rmsnorm(x, g):
    ms = jnp.mean(x * x, axis=-1, keepdims=True)
    return (x * lax.rsqrt(ms + NORM_EPS)) * g


def _inproj_kernel(x_ref, g_ref, w_ref, q_ref, k_ref, v_ref, xr_ref, gate_ref, *, qk_w, att_w, lru_w):
    h = _rmsnorm(x_ref[...], g_ref[...]).astype(BF16)

    def proj(lo, width):
        return jnp.dot(h, w_ref[:, lo:lo + width], preferred_element_type=F32)

    q_ref[...] = (proj(0, qk_w) * (LOG2E / math.sqrt(HEAD_DK))).astype(BF16)
    k_ref[...] = proj(qk_w, qk_w).astype(BF16)
    v_ref[...] = proj(2 * qk_w, att_w).astype(BF16)
    xr_ref[...] = proj(2 * qk_w + att_w, lru_w)
    gate_ref[...] = proj(2 * qk_w + att_w + lru_w, lru_w)


def _inproj(x2d, norm_g, w_in_bf16, *, tm):
    n_tok, d_model = x2d.shape
    qk_w = N_HEADS * 2 * HEAD_DK
    att_w = N_HEADS * HEAD_DV
    lru_w = (w_in_bf16.shape[1] - 2 * qk_w - att_w) // 2
    grid = (n_tok // tm,)
    row = lambda i: (i, 0)
    const = lambda i: (0, 0)
    return pl.pallas_call(
        functools.partial(_inproj_kernel, qk_w=qk_w, att_w=att_w, lru_w=lru_w),
        grid=grid,
        in_specs=[
            pl.BlockSpec((tm, d_model), row),
            pl.BlockSpec((1, d_model), const),
            pl.BlockSpec(w_in_bf16.shape, const, pipeline_mode=pl.Buffered(1)),
        ],
        out_specs=[
            pl.BlockSpec((tm, qk_w), row),
            pl.BlockSpec((tm, qk_w), row),
            pl.BlockSpec((tm, att_w), row),
            pl.BlockSpec((tm, lru_w), row),
            pl.BlockSpec((tm, lru_w), row),
        ],
        out_shape=[
            jax.ShapeDtypeStruct((n_tok, qk_w), BF16),
            jax.ShapeDtypeStruct((n_tok, qk_w), BF16),
            jax.ShapeDtypeStruct((n_tok, att_w), BF16),
            jax.ShapeDtypeStruct((n_tok, lru_w), F32),
            jax.ShapeDtypeStruct((n_tok, lru_w), F32),
        ],
        compiler_params=pltpu.CompilerParams(
            dimension_semantics=("arbitrary",), vmem_limit_bytes=VMEM_LIMIT),
        name="inproj",
    )(x2d, norm_g.reshape(1, d_model), w_in_bf16)


def _attn_kernel(slopes_ref, q_ref, k_ref, v_ref, lq1_ref, lk1_ref, lq2_ref, lk2_ref, g_ref,
                 o_ref, bias_ref, q2_ref, kfeat_ref, s_ref, mx_ref, p_ref, alpha_ref, m_ref, l_ref, acc_ref,
                 *, seq, tq, tk, lam_init):
    h = pl.program_id(0)
    b = pl.program_id(1)
    n_kc = seq // tk
    n_diag = tq // tk
    assert n_diag == 2 and n_kc >= n_diag and n_kc % 2 == 0
    slope2 = LOG2E * slopes_ref[h]

    def split3(x):
        hi = x.astype(BF16).astype(F32)
        mid = (x - hi).astype(BF16).astype(F32)
        return hi, mid, x - hi - mid

    def pieces(x, first):
        lane = lax.broadcasted_iota(jnp.int32, x.shape, x.ndim - 1)
        hi, mid, lo = split3(x)
        return jnp.where(lane == first, hi, jnp.where(lane == first + 1, mid,
                                                      jnp.where(lane == first + 2, lo, 0.0)))

    @pl.when(b == 0)
    def _():
        r = lax.broadcasted_iota(jnp.int32, (tq, tq), 0)
        u = lax.broadcasted_iota(jnp.int32, (tq, tq), 1)
        bias_ref[...] = -slope2 * jnp.abs(r - u).astype(F32)
        lane = lax.broadcasted_iota(jnp.int32, (tq, LANES), 1)
        rows = lax.broadcasted_iota(jnp.int32, (tq, LANES), 0).astype(F32)
        ones_q = jnp.where((lane >= 3) & (lane < 9), 1.0, 0.0)
        lane_k = lax.broadcasted_iota(jnp.int32, (tk, LANES), 1)
        cols = lax.broadcasted_iota(jnp.int32, (tk, LANES), 0).astype(F32)
        ones_k = jnp.where(lane_k < 3, 1.0, 0.0)
        for side, sg in enumerate((1.0, -1.0)):
            qf = (pieces(-slope2 * sg * rows, 0) + ones_q).astype(BF16)
            q2_ref[side, 0:tq, LANES:] = qf
            q2_ref[side, tq:2 * tq, LANES:] = qf
            kfeat_ref[side] = (pieces(slope2 * sg * cols, 3) + ones_k).astype(BF16)

    lam = (jnp.exp(jnp.sum(lq1_ref[...] * lk1_ref[...], axis=-1, keepdims=True))
           - jnp.exp(jnp.sum(lq2_ref[...] * lk2_ref[...], axis=-1, keepdims=True)) + lam_init)

    def put_scores(slot, s):
        tiles = [s[:, :, j * LANES:(j + 1) * LANES] for j in range(tk // LANES)]
        mx_ref[slot] = functools.reduce(jnp.maximum, tiles)
        s_ref[slot] = s

    def softmax(slot):
        tiles = [s_ref[slot, :, :, j * LANES:(j + 1) * LANES] for j in range(tk // LANES)]
        m_old = m_ref[...]
        m_new = jnp.maximum(m_old, jnp.max(mx_ref[slot], axis=-1, keepdims=True))
        alpha = jnp.exp2(m_old - m_new)
        l_new = alpha * l_ref[...]
        for j, s in enumerate(tiles):
            p = jnp.exp2(s - m_new)
            l_new = l_new + p
            p_ref[slot, :, :, j * LANES:(j + 1) * LANES] = p.astype(BF16)
        l_ref[...] = l_new
        m_ref[...] = m_new
        alpha_ref[slot] = alpha

    def q_block(qi):
        q = q_ref[0, qi * tq:(qi + 1) * tq, :]
        lane = lax.broadcasted_iota(jnp.int32, q.shape, 1)
        zero = jnp.zeros_like(q)
        for side in range(2):
            q2_ref[side, 0:tq, :LANES] = jnp.where(lane < HEAD_DK, q, zero)
            q2_ref[side, tq:2 * tq, :LANES] = jnp.where(lane >= HEAD_DK, q, zero)

        m_ref[...] = jnp.full(m_ref.shape, -jnp.inf, F32)
        l_ref[...] = jnp.zeros(l_ref.shape, F32)
        acc_ref[...] = jnp.zeros(acc_ref.shape, F32)

        def chunk_of(i):
            o = i - n_diag
            return jnp.where(i < n_diag, n_diag * qi + i, o + jnp.where(o >= n_diag * qi, n_diag, 0))

        def scores_diag(j, slot):
            ks = (n_diag * qi + j) * tk
            s = lax.dot_general(q2_ref[0, :, :LANES], k_ref[0, ks:ks + tk, :], (((1,), (1,)), ((), ())),
                                preferred_element_type=F32)
            put_scores(slot, s.reshape(2, tq, tk) + bias_ref[:, j * tk:(j + 1) * tk][None])

        def scores(i, slot):
            c = chunk_of(i)
            ks = pl.multiple_of(c * tk, tk)
            side = (c >= n_diag * qi).astype(jnp.int32)
            sg = 1.0 - 2.0 * side.astype(F32)
            delta = (qi * tq - c * tk).astype(F32)
            dvec = jnp.full((1, LANES), -slope2 * sg * delta, F32)
            kf = kfeat_ref[side] + pieces(dvec, 6).astype(BF16)
            rhs = jnp.concatenate([k_ref[0, pl.ds(ks, tk), :], kf], axis=1)
            s = lax.dot_general(q2_ref[side], rhs, (((1,), (1,)), ((), ())), preferred_element_type=F32)
            put_scores(slot, s.reshape(2, tq, tk))

        def values(i, slot):
            ks = pl.multiple_of(chunk_of(i) * tk, tk)
            pv = jnp.dot(p_ref[slot].reshape(2 * tq, tk), v_ref[0, pl.ds(ks, tk), :],
                         preferred_element_type=F32)
            acc_ref[...] = alpha_ref[slot] * acc_ref[...] + pv.reshape(2, tq, HEAD_DV)

        def step(i, slot):
            values(i - 1, 1 - slot)
            softmax(slot)
            scores(i + 1, 1 - slot)

        scores_diag(0, 0)
        scores_diag(1, 1)
        softmax(0)

        if n_kc > 2:
            def two_steps(n, _):
                step(2 * n + 1, 1)
                step(2 * n + 2, 0)
                return 0

            n_pairs = (n_kc - 2) // 2
            lax.fori_loop(0, jnp.minimum(h + n_pairs, n_pairs), two_steps, 0)
        values(n_kc - 2, n_kc % 2)
        softmax((n_kc - 1) % 2)
        values(n_kc - 1, (n_kc - 1) % 2)

        l = jnp.sum(l_ref[...], axis=-1, keepdims=True)
        o = acc_ref[0] / l[0] - lam * (acc_ref[1] / l[1])
        o_ref[0, qi * tq:(qi + 1) * tq, :] = (_rmsnorm(o, g_ref[...]) * (1.0 - lam_init)).astype(BF16)

    for qi in range(seq // tq):
        q_block(qi)


def _attention(q, k, v, lq1, lk1, lq2, lk2, subln_g, *, lam_init, tq, tk):
    batch, seq, _ = q.shape
    slopes = jnp.asarray([2.0 ** (-8.0 * (h + 1) / N_HEADS) for h in range(N_HEADS)], F32)
    grid = (N_HEADS, batch)
    vec = lambda a: a.reshape(1, -1).astype(F32)
    small = lambda n: pl.BlockSpec((1, n), lambda h, b: (0, 0))
    head = pl.BlockSpec((1, seq, HEAD_DV), lambda h, b: (b, 0, h))
    return pl.pallas_call(
        functools.partial(_attn_kernel, seq=seq, tq=tq, tk=tk, lam_init=lam_init),
        grid=grid,
        in_specs=[
            pl.BlockSpec(memory_space=pltpu.SMEM),
            head, head, head,
            small(HEAD_DK), small(HEAD_DK), small(HEAD_DK), small(HEAD_DK), small(HEAD_DV),
        ],
        out_specs=head,
        out_shape=jax.ShapeDtypeStruct((batch, seq, N_HEADS * HEAD_DV), BF16),
        scratch_shapes=[
            pltpu.VMEM((tq, tq), F32),
            pltpu.VMEM((2, 2 * tq, 2 * HEAD_DV), BF16),
            pltpu.VMEM((2, tk, LANES), BF16),
            pltpu.VMEM((2, 2, tq, tk), F32),
            pltpu.VMEM((2, 2, tq, LANES), F32),
            pltpu.VMEM((2, 2, tq, tk), BF16),
            pltpu.VMEM((2, 2, tq, LANES), F32),
            pltpu.VMEM((2, tq, LANES), F32),
            pltpu.VMEM((2, tq, LANES), F32),
            pltpu.VMEM((2, tq, HEAD_DV), F32),
        ],
        compiler_params=pltpu.CompilerParams(
            dimension_semantics=("arbitrary", "arbitrary"),
            vmem_limit_bytes=VMEM_LIMIT),
        name="diffattn",
    )(slopes, q, k, v, vec(lq1), vec(lk1), vec(lq2), vec(lk2), vec(subln_g))


def _softplus(x):
    return jnp.maximum(x, 0.0) + jnp.log1p(jnp.exp(-jnp.abs(x)))


def _lru_kernel(xr_ref, gate_ref, cw_ref, cb_ref, wg_ref, bg_ref, lam_ref, y_ref,
                xpad_ref, a_ref, b_ref, h_ref, *, seq, tc, n_grp):
    pad = SUBLANES

    xpad_ref[0:pad, :] = jnp.zeros((pad, LANES), F32)
    xpad_ref[pad + seq:pad + seq + pad, :] = jnp.zeros((pad, LANES), F32)
    xpad_ref[pad:pad + seq, :] = xr_ref[0]

    cw = cw_ref[...]
    cb = cb_ref[...]
    kk = _softplus(-lam_ref[0]) * (-0.5 * RG_C * LOG2E)

    def gates(c, _):
        t0 = pl.multiple_of(c * tc, tc)
        xc = cb
        for j in range(CONV_W):
            xc = xc + xpad_ref[pl.ds(t0 + pad - 2 + j, tc), :] * cw[j:j + 1, :]
        th = jnp.tanh(jnp.dot(xc.astype(BF16), wg_ref[0], preferred_element_type=F32) + bg_ref[0])
        xh = 0.5 * xc
        for d in range(2):
            t_r = th[:, (2 * d) * LANES:(2 * d + 1) * LANES]
            t_i = th[:, (2 * d + 1) * LANES:(2 * d + 2) * LANES]
            kd = kk[d:d + 1, :]
            a = jnp.exp2(kd * t_r + kd)
            m2 = jnp.maximum(1.0 - a * a, 1e-12)
            b = (m2 * lax.rsqrt(m2)) * (t_i * xh + xh)
            a_ref[d, pl.ds(t0, tc), :] = a
            b_ref[d, pl.ds(t0, tc), :] = b
        return 0

    part = seq // SUBLANES + 4
    assert (part // 4) % 2 == 1 and part % n_grp == 0
    seg = part // n_grp
    n_pad = SUBLANES * part - seq
    for d in range(2):
        a_ref[d, seq:seq + n_pad, :] = jnp.ones((n_pad, LANES), F32)
        b_ref[d, seq:seq + n_pad, :] = jnp.zeros((n_pad, LANES), F32)

    def tile(g, t):
        return pl.ds(g * seg + t, SUBLANES, stride=part)

    lax.fori_loop(0, seq // tc, gates, 0, unroll=2)

    zeros = jnp.zeros((SUBLANES, LANES), F32)
    ones = jnp.ones((SUBLANES, LANES), F32)

    def pass1(n, carry):
        out = []
        for d in range(2):
            t = n if d == 0 else seg - 1 - n
            hs, ps = carry[d]
            nh, npr = [], []
            for g in range(n_grp):
                a = a_ref[d, tile(g, t), :]
                nh.append(a * hs[g] + b_ref[d, tile(g, t), :])
                npr.append(a * ps[g])
            out.append((tuple(nh), tuple(npr)))
        return tuple(out)

    ends = lax.fori_loop(0, seg, pass1,
                         tuple(((zeros,) * n_grp, (ones,) * n_grp) for _ in range(2)), unroll=2)

    def chain_rows(e, q, reverse):
        row = lax.broadcasted_iota(jnp.int32, (SUBLANES, LANES), 0)
        edge = (SUBLANES - 1) if reverse else 0
        shift = (SUBLANES - 1) if reverse else 1
        carry = zeros
        for _ in range(SUBLANES - 1):
            carry = jnp.where(row == edge, 0.0, pltpu.roll(e + q * carry, shift, 0))
        return carry

    def entry_states(h_end, p_end, reverse):
        order = range(n_grp - 1, -1, -1) if reverse else range(n_grp)
        local, prod = {}, {}
        c, q = zeros, ones
        for g in order:
            local[g], prod[g] = c, q
            c = h_end[g] + p_end[g] * c
            q = p_end[g] * q
        row_in = chain_rows(c, q, reverse)
        return tuple(local[g] + prod[g] * row_in for g in range(n_grp))

    def pass2(n, carry):
        out = []
        for d in range(2):
            t = n if d == 0 else seg - 1 - n
            nh = []
            for g in range(n_grp):
                hcur = a_ref[d, tile(g, t), :] * carry[d][g] + b_ref[d, tile(g, t), :]
                h_ref[d, tile(g, t), :] = hcur
                nh.append(hcur)
            out.append(tuple(nh))
        return tuple(out)

    lax.fori_loop(0, seg, pass2,
                  tuple(entry_states(ends[d][0], ends[d][1], d == 1) for d in range(2)), unroll=2)

    c1 = math.sqrt(2.0 / math.pi)

    def finish(c, _):
        t0 = pl.multiple_of(c * tc, tc)
        hsum = h_ref[0, pl.ds(t0, tc), :] + h_ref[1, pl.ds(t0, tc), :]
        x = gate_ref[0, pl.ds(t0, tc), :]
        th = jnp.tanh(x * ((x * x) * (c1 * 0.044715) + c1))
        xh = 0.5 * x
        y_ref[0, pl.ds(t0, tc), :] = (hsum * (xh * th + xh)).astype(BF16)
        return 0

    lax.fori_loop(0, seq // tc, finish, 0)


def _lru_gate_weights(w_rg, b_rg, w_ig, b_ig):
    blk = w_rg.shape[-1]
    per = LANES // blk
    n_groups = N_LRU_BLOCKS // per

    def dense(w):
        w = w.reshape(n_groups, per, blk, blk)
        eye = jnp.eye(per, dtype=w.dtype)
        return jnp.einsum('gpcd,pq->gpcqd', w, eye).reshape(n_groups, LANES, LANES)

    wg = jnp.concatenate([dense(w_rg[0]), dense(w_ig[0]), dense(w_rg[1]), dense(w_ig[1])], axis=-1)
    bias = lambda v: v.reshape(n_groups, 1, LANES)
    bg = jnp.concatenate([bias(b_rg[0]), bias(b_ig[0]), bias(b_rg[1]), bias(b_ig[1])], axis=-1)
    return (0.5 * wg).astype(BF16), (0.5 * bg).astype(F32)


def _rglru(xr, gate, conv_w, conv_b, wg, bg, lru_lambda, *, tc, n_grp):
    batch, seq, lru_w = xr.shape
    n_groups = lru_w // LANES
    lam = lru_lambda.reshape(2, n_groups, LANES).transpose(1, 0, 2)
    grid = (batch, n_groups)
    tok = pl.BlockSpec((1, seq, LANES), lambda b, j: (b, 0, j))
    return pl.pallas_call(
        functools.partial(_lru_kernel, seq=seq, tc=tc, n_grp=n_grp),
        grid=grid,
        in_specs=[
            tok, tok,
            pl.BlockSpec((CONV_W, LANES), lambda b, j: (0, j)),
            pl.BlockSpec((1, LANES), lambda b, j: (0, j)),
            pl.BlockSpec((1, LANES, 4 * LANES), lambda b, j: (j, 0, 0)),
            pl.BlockSpec((1, 1, 4 * LANES), lambda b, j: (j, 0, 0)),
            pl.BlockSpec((1, 2, LANES), lambda b, j: (j, 0, 0)),
        ],
        out_specs=tok,
        out_shape=jax.ShapeDtypeStruct((batch, seq, lru_w), BF16),
        scratch_shapes=[
            pltpu.VMEM((seq + 2 * SUBLANES, LANES), F32),
            pltpu.VMEM((2, seq + 4 * SUBLANES, LANES), F32),
            pltpu.VMEM((2, seq + 4 * SUBLANES, LANES), F32),
            pltpu.VMEM((2, seq + 4 * SUBLANES, LANES), F32),
        ],
        compiler_params=pltpu.CompilerParams(
            dimension_semantics=("arbitrary", "arbitrary"), vmem_limit_bytes=VMEM_LIMIT),
        name="rglru",
    )(xr, gate, conv_w, conv_b.reshape(1, lru_w), wg, bg, lam)


def _ffn_kernel(x_ref, o_ref, y_ref, wo_ref, g2_ref, wg_ref, wu_ref, wd_ref, g3_ref, out_ref, *, att_w):
    x1 = (x_ref[...]
          + jnp.dot(o_ref[...], wo_ref[0:att_w, :], preferred_element_type=F32)
          + jnp.dot(y_ref[...], wo_ref[att_w:, :], preferred_element_type=F32))
    h2 = _rmsnorm(x1, g2_ref[...]).astype(BF16)
    gt = jnp.dot(h2, wg_ref[...], preferred_element_type=F32)
    up = jnp.dot(h2, wu_ref[...], preferred_element_type=F32)
    gh = 0.5 * gt
    act = ((gh * jnp.tanh(gh) + gh) * up).astype(BF16)
    x2 = x1 + jnp.dot(act, wd_ref[...], preferred_element_type=F32)
    out_ref[...] = _rmsnorm(x2, g3_ref[...])


def _out_ffn(x2d, o2d, y2d, w_out, norm_ffn, w_gate, w_up, w_down, norm_final, *, tm):
    n_tok, d_model = x2d.shape
    att_w = o2d.shape[1]
    lru_w = y2d.shape[1]
    row = lambda i: (i, 0)
    const = lambda i: (0, 0)
    resident = lambda a: pl.BlockSpec(a.shape, const, pipeline_mode=pl.Buffered(1))
    g2 = norm_ffn.reshape(1, d_model)
    g3 = norm_final.reshape(1, d_model)
    return pl.pallas_call(
        functools.partial(_ffn_kernel, att_w=att_w),
        grid=(n_tok // tm,),
        in_specs=[
            pl.BlockSpec((tm, d_model), row),
            pl.BlockSpec((tm, att_w), row),
            pl.BlockSpec((tm, lru_w), row),
            resident(w_out), resident(g2), resident(w_gate), resident(w_up), resident(w_down),
            resident(g3),
        ],
        out_specs=pl.BlockSpec((tm, d_model), row),
        out_shape=jax.ShapeDtypeStruct((n_tok, d_model), F32),
        compiler_params=pltpu.CompilerParams(
            dimension_semantics=("arbitrary",), vmem_limit_bytes=VMEM_LIMIT),
        name="out_ffn",
    )(x2d, o2d, y2d, w_out, g2, w_gate, w_up, w_down, g3)


def _trunk(x, p, *, tm_in, tq, tk, tc, n_grp, tm_ffn):
    batch, seq, d_model = x.shape
    x2d = x.reshape(batch * seq, d_model)
    depth = p["w_in"].shape[0]
    for l in range(depth):
        last = l == depth - 1
        lam_init = 0.8 - 0.6 * math.exp(-0.3 * l)
        q, k, v, xr, gate = _inproj(x2d, p["norm_mix"][l], p["w_in"][l].astype(BF16), tm=tm_in)
        att_w = v.shape[1]
        lru_w = xr.shape[1]
        shp = lambda a: a.reshape(batch, seq, a.shape[1])
        o = _attention(shp(q), shp(k), shp(v), p["lambda_q1"][l], p["lambda_k1"][l],
                       p["lambda_q2"][l], p["lambda_k2"][l], p["subln_g"][l],
                       lam_init=lam_init, tq=tq, tk=tk)
        wg, bg = _lru_gate_weights(p["w_rg"][l], p["b_rg"][l], p["w_ig"][l], p["b_ig"][l])
        y = _rglru(shp(xr), shp(gate), p["conv_w"][l], p["conv_b"][l], wg, bg, p["lru_lambda"][l],
                   tc=tc, n_grp=n_grp)
        assert last, "only the last layer's fused final norm is implemented"
        x2d = _out_ffn(x2d, o.reshape(batch * seq, att_w), y.reshape(batch * seq, lru_w),
                       p["w_out"][l].astype(BF16), p["norm_ffn"][l], p["w_gate"][l].astype(BF16),
                       p["w_up"][l].astype(BF16), p["w_down"][l].astype(BF16), p["norm_final"],
                       tm=tm_ffn)
    return x2d.reshape(batch, seq, d_model)


def _tiles(seq):
    return dict(tm_in=1024, tq=min(1024, seq), tk=min(512, seq), tc=min(256, seq // SUBLANES), n_grp=4,
                tm_ffn=512)


def kernel(x_prompt, x_sample, norm_mix, w_in, conv_w, conv_b, w_rg, b_rg, w_ig, b_ig, lru_lambda, lambda_q1, lambda_k1, lambda_q2, lambda_k2, subln_g, w_out, norm_ffn, w_gate, w_up, w_down, norm_final):
    p = dict(norm_mix=norm_mix, w_in=w_in, conv_w=conv_w, conv_b=conv_b, w_rg=w_rg, b_rg=b_rg,
             w_ig=w_ig, b_ig=b_ig, lru_lambda=lru_lambda, lambda_q1=lambda_q1, lambda_k1=lambda_k1,
             lambda_q2=lambda_q2, lambda_k2=lambda_k2, subln_g=subln_g, w_out=w_out,
             norm_ffn=norm_ffn, w_gate=w_gate, w_up=w_up, w_down=w_down, norm_final=norm_final)
    y_prompt = _trunk(x_prompt, p, **_tiles(x_prompt.shape[1]))
    y_sample = _trunk(x_sample, p, **_tiles(x_sample.shape[1]))
    return (y_prompt, y_sample)
```

```python
import functools
import math

import jax
import jax.numpy as jnp
from jax import lax
from jax.experimental import pallas as pl
from jax.experimental.pallas import tpu as pltpu

F32 = jnp.float32
BF16 = jnp.bfloat16

N_HEADS = 4
HEAD_DV = 128
HEAD_DK = 64
N_LRU_BLOCKS = 8
CONV_W = 4
RG_C = 8.0
NORM_EPS = 1e-6
LOG2E = math.log2(math.e)
LANES = 128
SUBLANES = 8
VMEM_LIMIT = 56 * 1024 * 1024


def _rmsnorm(x, g):
    ms = jnp.mean(x * x, axis=-1, keepdims=True)
    return (x * lax.rsqrt(ms + NORM_EPS)) * g


def _inproj_kernel(x_ref, g_ref, w_ref, q_ref, k_ref, v_ref, xr_ref, gate_ref, *, qk_w, att_w, lru_w):
    h = _rmsnorm(x_ref[...], g_ref[...]).astype(BF16)

    def proj(lo, width):
        return jnp.dot(h, w_ref[:, lo:lo + width], preferred_element_type=F32)

    q_ref[...] = (proj(0, qk_w) * (LOG2E / math.sqrt(HEAD_DK))).astype(BF16)
    k_ref[...] = proj(qk_w, qk_w).astype(BF16)
    v_ref[...] = proj(2 * qk_w, att_w).astype(BF16)
    xr_ref[...] = proj(2 * qk_w + att_w, lru_w)
    gate_ref[...] = proj(2 * qk_w + att_w + lru_w, lru_w)


def _inproj(x2d, norm_g, w_in_bf16, *, tm):
    n_tok, d_model = x2d.shape
    qk_w = N_HEADS * 2 * HEAD_DK
    att_w = N_HEADS * HEAD_DV
    lru_w = (w_in_bf16.shape[1] - 2 * qk_w - att_w) // 2
    grid = (n_tok // tm,)
    row = lambda i: (i, 0)
    const = lambda i: (0, 0)
    return pl.pallas_call(
        functools.partial(_inproj_kernel, qk_w=qk_w, att_w=att_w, lru_w=lru_w),
        grid=grid,
        in_specs=[
            pl.BlockSpec((tm, d_model), row),
            pl.BlockSpec((1, d_model), const),
            pl.BlockSpec(w_in_bf16.shape, const, pipeline_mode=pl.Buffered(1)),
        ],
        out_specs=[
            pl.BlockSpec((tm, qk_w), row),
            pl.BlockSpec((tm, qk_w), row),
            pl.BlockSpec((tm, att_w), row),
            pl.BlockSpec((tm, lru_w), row),
            pl.BlockSpec((tm, lru_w), row),
        ],
        out_shape=[
            jax.ShapeDtypeStruct((n_tok, qk_w), BF16),
            jax.ShapeDtypeStruct((n_tok, qk_w), BF16),
            jax.ShapeDtypeStruct((n_tok, att_w), BF16),
            jax.ShapeDtypeStruct((n_tok, lru_w), F32),
            jax.ShapeDtypeStruct((n_tok, lru_w), F32),
        ],
        compiler_params=pltpu.CompilerParams(
            dimension_semantics=("arbitrary",), vmem_limit_bytes=VMEM_LIMIT),
        name="inproj",
    )(x2d, norm_g.reshape(1, d_model), w_in_bf16)


def _attn_kernel(slopes_ref, q_ref, k_ref, v_ref, lq1_ref, lk1_ref, lq2_ref, lk2_ref, g_ref,
                 o_ref, bias_ref, q2_ref, kfeat_ref, s_ref, mx_ref, p_ref, alpha_ref, m_ref, l_ref, acc_ref,
                 *, seq, tq, tk, lam_init):
    h = pl.program_id(0)
    b = pl.program_id(1)
    n_kc = seq // tk
    n_diag = tq // tk
    assert n_diag == 2 and n_kc >= n_diag and n_kc % 2 == 0
    slope2 = LOG2E * slopes_ref[h]

    def split3(x):
        hi = x.astype(BF16).astype(F32)
        mid = (x - hi).astype(BF16).astype(F32)
        return hi, mid, x - hi - mid

    def pieces(x, first):
        lane = lax.broadcasted_iota(jnp.int32, x.shape, x.ndim - 1)
        hi, mid, lo = split3(x)
        return jnp.where(lane == first, hi, jnp.where(lane == first + 1, mid,
                                                      jnp.where(lane == first + 2, lo, 0.0)))

    @pl.when(b == 0)
    def _():
        r = lax.broadcasted_iota(jnp.int32, (tq, tq), 0)
        u = lax.broadcasted_iota(jnp.int32, (tq, tq), 1)
        bias_ref[...] = -slope2 * jnp.abs(r - u).astype(F32)
        lane = lax.broadcasted_iota(jnp.int32, (tq, LANES), 1)
        rows = lax.broadcasted_iota(jnp.int32, (tq, LANES), 0).astype(F32)
        ones_q = jnp.where((lane >= 3) & (lane < 9), 1.0, 0.0)
        lane_k = lax.broadcasted_iota(jnp.int32, (tk, LANES), 1)
        cols = lax.broadcasted_iota(jnp.int32, (tk, LANES), 0).astype(F32)
        ones_k = jnp.where(lane_k < 3, 1.0, 0.0)
        for side, sg in enumerate((1.0, -1.0)):
            qf = (pieces(-slope2 * sg * rows, 0) + ones_q).astype(BF16)
            q2_ref[side, 0:tq, LANES:] = qf
            q2_ref[side, tq:2 * tq, LANES:] = qf
            kfeat_ref[side] = (pieces(slope2 * sg * cols, 3) + ones_k).astype(BF16)

    lam = (jnp.exp(jnp.sum(lq1_ref[...] * lk1_ref[...], axis=-1, keepdims=True))
           - jnp.exp(jnp.sum(lq2_ref[...] * lk2_ref[...], axis=-1, keepdims=True)) + lam_init)

    def put_scores(slot, s):
        tiles = [s[:, :, j * LANES:(j + 1) * LANES] for j in range(tk // LANES)]
        mx_ref[slot] = functools.reduce(jnp.maximum, tiles)
        s_ref[slot] = s

    def softmax(slot, first=False):
        tiles = [s_ref[slot, :, :, j * LANES:(j + 1) * LANES] for j in range(tk // LANES)]
        m_new = jnp.max(mx_ref[slot], axis=-1, keepdims=True)
        if first:
            m_new = jnp.broadcast_to(m_new, m_ref.shape)
            alpha = jnp.zeros(m_ref.shape, F32)
            l_new = alpha
        else:
            m_old = m_ref[...]
            m_new = jnp.maximum(m_old, m_new)
            alpha = jnp.exp2(m_old - m_new)
            l_new = alpha * l_ref[...]
        for j, s in enumerate(tiles):
            p = jnp.exp2(s - m_new)
            l_new = l_new + p
            p_ref[slot, :, :, j * LANES:(j + 1) * LANES] = p.astype(BF16)
        l_ref[...] = l_new
        m_ref[...] = m_new
        alpha_ref[slot] = alpha

    def q_block(qi):
        q = q_ref[0, qi * tq:(qi + 1) * tq, :]
        lane = lax.broadcasted_iota(jnp.int32, q.shape, 1)
        zero = jnp.zeros_like(q)
        for side in range(2):
            q2_ref[side, 0:tq, :LANES] = jnp.where(lane < HEAD_DK, q, zero)
            q2_ref[side, tq:2 * tq, :LANES] = jnp.where(lane >= HEAD_DK, q, zero)

        acc_ref[...] = jnp.zeros(acc_ref.shape, F32)

        def chunk_of(i):
            o = i - n_diag
            return jnp.where(i < n_diag, n_diag * qi + i, o + jnp.where(o >= n_diag * qi, n_diag, 0))

        def scores_diag(j, slot):
            ks = (n_diag * qi + j) * tk
            s = lax.dot_general(q2_ref[0, :, :LANES], k_ref[0, ks:ks + tk, :], (((1,), (1,)), ((), ())),
                                preferred_element_type=F32)
            put_scores(slot, s.reshape(2, tq, tk) + bias_ref[:, j * tk:(j + 1) * tk][None])

        def scores(i, slot):
            c = chunk_of(i)
            ks = pl.multiple_of(c * tk, tk)
            side = (c >= n_diag * qi).astype(jnp.int32)
            sg = 1.0 - 2.0 * side.astype(F32)
            delta = (qi * tq - c * tk).astype(F32)
            dvec = jnp.full((1, LANES), -slope2 * sg * delta, F32)
            kf = kfeat_ref[side] + pieces(dvec, 6).astype(BF16)
            rhs = jnp.concatenate([k_ref[0, pl.ds(ks, tk), :], kf], axis=1)
            s = lax.dot_general(q2_ref[side], rhs, (((1,), (1,)), ((), ())), preferred_element_type=F32)
            put_scores(slot, s.reshape(2, tq, tk))

        def values(i, slot):
            ks = pl.multiple_of(chunk_of(i) * tk, tk)
            pv = jnp.dot(p_ref[slot].reshape(2 * tq, tk), v_ref[0, pl.ds(ks, tk), :],
                         preferred_element_type=F32)
            acc_ref[...] = alpha_ref[slot] * acc_ref[...] + pv.reshape(2, tq, HEAD_DV)

        def step(i, slot):
            values(i - 1, 1 - slot)
            softmax(slot)
            scores(i + 1, 1 - slot)

        scores_diag(0, 0)
        scores_diag(1, 1)
        softmax(0, first=True)

        if n_kc > 2:
            def two_steps(n, _):
                step(2 * n + 1, 1)
                step(2 * n + 2, 0)
                return 0

            n_pairs = (n_kc - 2) // 2
            lax.fori_loop(0, jnp.minimum(h + n_pairs, n_pairs), two_steps, 0)
        values(n_kc - 2, n_kc % 2)
        softmax((n_kc - 1) % 2)
        values(n_kc - 1, (n_kc - 1) % 2)

        l = jnp.sum(l_ref[...], axis=-1, keepdims=True)
        o = acc_ref[0] / l[0] - lam * (acc_ref[1] / l[1])
        o_ref[0, qi * tq:(qi + 1) * tq, :] = (_rmsnorm(o, g_ref[...]) * (1.0 - lam_init)).astype(BF16)

    for qi in range(seq // tq):
        q_block(qi)


def _attention(q, k, v, lq1, lk1, lq2, lk2, subln_g, *, lam_init, tq, tk):
    batch, seq, _ = q.shape
    slopes = jnp.asarray([2.0 ** (-8.0 * (h + 1) / N_HEADS) for h in range(N_HEADS)], F32)
    grid = (N_HEADS, batch)
    vec = lambda a: a.reshape(1, -1).astype(F32)
    small = lambda n: pl.BlockSpec((1, n), lambda h, b: (0, 0))
    head = pl.BlockSpec((1, seq, HEAD_DV), lambda h, b: (b, 0, h))
    return pl.pallas_call(
        functools.partial(_attn_kernel, seq=seq, tq=tq, tk=tk, lam_init=lam_init),
        grid=grid,
        in_specs=[
            pl.BlockSpec(memory_space=pltpu.SMEM),
            head, head, head,
            small(HEAD_DK), small(HEAD_DK), small(HEAD_DK), small(HEAD_DK), small(HEAD_DV),
        ],
        out_specs=head,
        out_shape=jax.ShapeDtypeStruct((batch, seq, N_HEADS * HEAD_DV), BF16),
        scratch_shapes=[
            pltpu.VMEM((tq, tq), F32),
            pltpu.VMEM((2, 2 * tq, 2 * HEAD_DV), BF16),
            pltpu.VMEM((2, tk, LANES), BF16),
            pltpu.VMEM((2, 2, tq, tk), F32),
            pltpu.VMEM((2, 2, tq, LANES), F32),
            pltpu.VMEM((2, 2, tq, tk), BF16),
            pltpu.VMEM((2, 2, tq, LANES), F32),
            pltpu.VMEM((2, tq, LANES), F32),
            pltpu.VMEM((2, tq, LANES), F32),
            pltpu.VMEM((2, tq, HEAD_DV), F32),
        ],
        compiler_params=pltpu.CompilerParams(
            dimension_semantics=("arbitrary", "arbitrary"),
            vmem_limit_bytes=VMEM_LIMIT),
        name="diffattn",
    )(slopes, q, k, v, vec(lq1), vec(lk1), vec(lq2), vec(lk2), vec(subln_g))


def _softplus(x):
    return jnp.maximum(x, 0.0) + jnp.log1p(jnp.exp(-jnp.abs(x)))


def _lru_kernel(xr_ref, gate_ref, cw_ref, cb_ref, wg_ref, bg_ref, lam_ref, y_ref,
                xpad_ref, a_ref, b_ref, h_ref, *, seq, tc, n_grp):
    pad = SUBLANES

    xpad_ref[0:pad, :] = jnp.zeros((pad, LANES), F32)
    xpad_ref[pad + seq:pad + seq + pad, :] = jnp.zeros((pad, LANES), F32)
    xpad_ref[pad:pad + seq, :] = xr_ref[0]

    cw = cw_ref[...]
    cb = cb_ref[...]
    kk = _softplus(-lam_ref[0]) * (-0.5 * RG_C * LOG2E)

    def gates(c, _):
        t0 = pl.multiple_of(c * tc, tc)
        xc = cb
        for j in range(CONV_W):
            xc = xc + xpad_ref[pl.ds(t0 + pad - 2 + j, tc), :] * cw[j:j + 1, :]
        th = jnp.tanh(jnp.dot(xc.astype(BF16), wg_ref[0], preferred_element_type=F32) + bg_ref[0])
        xh = 0.5 * xc
        for d in range(2):
            t_r = th[:, (2 * d) * LANES:(2 * d + 1) * LANES]
            t_i = th[:, (2 * d + 1) * LANES:(2 * d + 2) * LANES]
            kd = kk[d:d + 1, :]
            a = jnp.exp2(kd * t_r + kd)
            m2 = jnp.maximum(1.0 - a * a, 1e-12)
            b = (m2 * lax.rsqrt(m2)) * (t_i * xh + xh)
            a_ref[d, pl.ds(t0, tc), :] = a
            b_ref[d, pl.ds(t0, tc), :] = b
        return 0

    part = seq // SUBLANES + 4
    assert (part // 4) % 2 == 1 and part % n_grp == 0
    seg = part // n_grp
    n_pad = SUBLANES * part - seq
    for d in range(2):
        a_ref[d, seq:seq + n_pad, :] = jnp.ones((n_pad, LANES), F32)
        b_ref[d, seq:seq + n_pad, :] = jnp.zeros((n_pad, LANES), F32)

    def tile(g, t):
        return pl.ds(g * seg + t, SUBLANES, stride=part)

    lax.fori_loop(0, seq // tc, gates, 0, unroll=2)

    zeros = jnp.zeros((SUBLANES, LANES), F32)
    ones = jnp.ones((SUBLANES, LANES), F32)

    def pass1(n, carry):
        out = []
        for d in range(2):
            t = n if d == 0 else seg - 1 - n
            hs, ps = carry[d]
            nh, npr = [], []
            for g in range(n_grp):
                a = a_ref[d, tile(g, t), :]
                nh.append(a * hs[g] + b_ref[d, tile(g, t), :])
                npr.append(a * ps[g])
            out.append((tuple(nh), tuple(npr)))
        return tuple(out)

    ends = lax.fori_loop(0, seg, pass1,
                         tuple(((zeros,) * n_grp, (ones,) * n_grp) for _ in range(2)), unroll=2)

    def chain_rows(e, q, reverse):
        row = lax.broadcasted_iota(jnp.int32, (SUBLANES, LANES), 0)
        edge = (SUBLANES - 1) if reverse else 0
        shift = (SUBLANES - 1) if reverse else 1
        carry = zeros
        for _ in range(SUBLANES - 1):
            carry = jnp.where(row == edge, 0.0, pltpu.roll(e + q * carry, shift, 0))
        return carry

    def entry_states(h_end, p_end, reverse):
        order = range(n_grp - 1, -1, -1) if reverse else range(n_grp)
        local, prod = {}, {}
        c, q = zeros, ones
        for g in order:
            local[g], prod[g] = c, q
            c = h_end[g] + p_end[g] * c
            q = p_end[g] * q
        row_in = chain_rows(c, q, reverse)
        return tuple(local[g] + prod[g] * row_in for g in range(n_grp))

    def pass2(n, carry):
        out = []
        for d in range(2):
            t = n if d == 0 else seg - 1 - n
            nh = []
            for g in range(n_grp):
                hcur = a_ref[d, tile(g, t), :] * carry[d][g] + b_ref[d, tile(g, t), :]
                h_ref[d, tile(g, t), :] = hcur
                nh.append(hcur)
            out.append(tuple(nh))
        return tuple(out)

    lax.fori_loop(0, seg, pass2,
                  tuple(entry_states(ends[d][0], ends[d][1], d == 1) for d in range(2)), unroll=2)

    c1 = math.sqrt(2.0 / math.pi)

    def finish(c, _):
        t0 = pl.multiple_of(c * tc, tc)
        hsum = h_ref[0, pl.ds(t0, tc), :] + h_ref[1, pl.ds(t0, tc), :]
        x = gate_ref[0, pl.ds(t0, tc), :]
        th = jnp.tanh(x * ((x * x) * (c1 * 0.044715) + c1))
        xh = 0.5 * x
        y_ref[0, pl.ds(t0, tc), :] = (hsum * (xh * th + xh)).astype(BF16)
        return 0

    lax.fori_loop(0, seq // tc, finish, 0)


def _lru_gate_weights(w_rg, b_rg, w_ig, b_ig):
    blk = w_rg.shape[-1]
    per = LANES // blk
    n_groups = N_LRU_BLOCKS // per

    def dense(w):
        w = w.reshape(n_groups, per, blk, blk)
        eye = jnp.eye(per, dtype=w.dtype)
        return jnp.einsum('gpcd,pq->gpcqd', w, eye).reshape(n_groups, LANES, LANES)

    wg = jnp.concatenate([dense(w_rg[0]), dense(w_ig[0]), dense(w_rg[1]), dense(w_ig[1])], axis=-1)
    bias = lambda v: v.reshape(n_groups, 1, LANES)
    bg = jnp.concatenate([bias(b_rg[0]), bias(b_ig[0]), bias(b_rg[1]), bias(b_ig[1])], axis=-1)
    return (0.5 * wg).astype(BF16), (0.5 * bg).astype(F32)


def _rglru(xr, gate, conv_w, conv_b, wg, bg, lru_lambda, *, tc, n_grp):
    batch, seq, lru_w = xr.shape
    n_groups = lru_w // LANES
    lam = lru_lambda.reshape(2, n_groups, LANES).transpose(1, 0, 2)
    grid = (batch, n_groups)
    tok = pl.BlockSpec((1, seq, LANES), lambda b, j: (b, 0, j))
    return pl.pallas_call(
        functools.partial(_lru_kernel, seq=seq, tc=tc, n_grp=n_grp),
        grid=grid,
        in_specs=[
            tok, tok,
            pl.BlockSpec((CONV_W, LANES), lambda b, j: (0, j)),
            pl.BlockSpec((1, LANES), lambda b, j: (0, j)),
            pl.BlockSpec((1, LANES, 4 * LANES), lambda b, j: (j, 0, 0)),
            pl.BlockSpec((1, 1, 4 * LANES), lambda b, j: (j, 0, 0)),
            pl.BlockSpec((1, 2, LANES), lambda b, j: (j, 0, 0)),
        ],
        out_specs=tok,
        out_shape=jax.ShapeDtypeStruct((batch, seq, lru_w), BF16),
        scratch_shapes=[
            pltpu.VMEM((seq + 2 * SUBLANES, LANES), F32),
            pltpu.VMEM((2, seq + 4 * SUBLANES, LANES), F32),
            pltpu.VMEM((2, seq + 4 * SUBLANES, LANES), F32),
            pltpu.VMEM((2, seq + 4 * SUBLANES, LANES), F32),
        ],
        compiler_params=pltpu.CompilerParams(
            dimension_semantics=("arbitrary", "arbitrary"), vmem_limit_bytes=VMEM_LIMIT),
        name="rglru",
    )(xr, gate, conv_w, conv_b.reshape(1, lru_w), wg, bg, lam)


def _ffn_kernel(x_ref, o_ref, y_ref, wo_ref, g2_ref, wg_ref, wu_ref, wd_ref, g3_ref, out_ref, *, att_w):
    mix = jnp.concatenate([o_ref[...], y_ref[...]], axis=1)
    x1 = x_ref[...] + jnp.dot(mix, wo_ref[...], preferred_element_type=F32)
    h2 = _rmsnorm(x1, g2_ref[...]).astype(BF16)
    gt = jnp.dot(h2, wg_ref[...], preferred_element_type=F32)
    up = jnp.dot(h2, wu_ref[...], preferred_element_type=F32)
    gh = 0.5 * gt
    act = ((gh * jnp.tanh(gh) + gh) * up).astype(BF16)
    x2 = x1 + jnp.dot(act, wd_ref[...], preferred_element_type=F32)
    out_ref[...] = _rmsnorm(x2, g3_ref[...])


def _out_ffn(x2d, o2d, y2d, w_out, norm_ffn, w_gate, w_up, w_down, norm_final, *, tm):
    n_tok, d_model = x2d.shape
    att_w = o2d.shape[1]
    lru_w = y2d.shape[1]
    row = lambda i: (i, 0)
    const = lambda i: (0, 0)
    resident = lambda a: pl.BlockSpec(a.shape, const, pipeline_mode=pl.Buffered(1))
    g2 = norm_ffn.reshape(1, d_model)
    g3 = norm_final.reshape(1, d_model)
    return pl.pallas_call(
        functools.partial(_ffn_kernel, att_w=att_w),
        grid=(n_tok // tm,),
        in_specs=[
            pl.BlockSpec((tm, d_model), row),
            pl.BlockSpec((tm, att_w), row),
            pl.BlockSpec((tm, lru_w), row),
            resident(w_out), resident(g2), resident(w_gate), resident(w_up), resident(w_down),
            resident(g3),
        ],
        out_specs=pl.BlockSpec((tm, d_model), row),
        out_shape=jax.ShapeDtypeStruct((n_tok, d_model), F32),
        compiler_params=pltpu.CompilerParams(
            dimension_semantics=("arbitrary",), vmem_limit_bytes=VMEM_LIMIT),
        name="out_ffn",
    )(x2d, o2d, y2d, w_out, g2, w_gate, w_up, w_down, g3)


def _trunk(x, p, *, tm_in, tq, tk, tc, n_grp, tm_ffn):
    batch, seq, d_model = x.shape
    x2d = x.reshape(batch * seq, d_model)
    depth = p["w_in"].shape[0]
    for l in range(depth):
        last = l == depth - 1
        lam_init = 0.8 - 0.6 * math.exp(-0.3 * l)
        q, k, v, xr, gate = _inproj(x2d, p["norm_mix"][l], p["w_in"][l].astype(BF16), tm=tm_in)
        att_w = v.shape[1]
        lru_w = xr.shape[1]
        shp = lambda a: a.reshape(batch, seq, a.shape[1])
        o = _attention(shp(q), shp(k), shp(v), p["lambda_q1"][l], p["lambda_k1"][l],
                       p["lambda_q2"][l], p["lambda_k2"][l], p["subln_g"][l],
                       lam_init=lam_init, tq=tq, tk=tk)
        wg, bg = _lru_gate_weights(p["w_rg"][l], p["b_rg"][l], p["w_ig"][l], p["b_ig"][l])
        y = _rglru(shp(xr), shp(gate), p["conv_w"][l], p["conv_b"][l], wg, bg, p["lru_lambda"][l],
                   tc=tc, n_grp=n_grp)
        assert last, "only the last layer's fused final norm is implemented"
        x2d = _out_ffn(x2d, o.reshape(batch * seq, att_w), y.reshape(batch * seq, lru_w),
                       p["w_out"][l].astype(BF16), p["norm_ffn"][l], p["w_gate"][l].astype(BF16),
                       p["w_up"][l].astype(BF16), p["w_down"][l].astype(BF16), p["norm_final"],
                       tm=tm_ffn)
    return x2d.reshape(batch, seq, d_model)


def _tiles(seq):
    return dict(tm_in=1024, tq=min(1024, seq), tk=min(512, seq), tc=min(256, seq // SUBLANES), n_grp=4,
                tm_ffn=512)


def kernel(x_prompt, x_sample, norm_mix, w_in, conv_w, conv_b, w_rg, b_rg, w_ig, b_ig, lru_lambda, lambda_q1, lambda_k1, lambda_q2, lambda_k2, subln_g, w_out, norm_ffn, w_gate, w_up, w_down, norm_final):
    p = dict(norm_mix=norm_mix, w_in=w_in, conv_w=conv_w, conv_b=conv_b, w_rg=w_rg, b_rg=b_rg,
             w_ig=w_ig, b_ig=b_ig, lru_lambda=lru_lambda, lambda_q1=lambda_q1, lambda_k1=lambda_k1,
             lambda_q2=lambda_q2, lambda_k2=lambda_k2, subln_g=subln_g, w_out=w_out,
             norm_ffn=norm_ffn, w_gate=w_gate, w_up=w_up, w_down=w_down, norm_final=norm_final)
    y_prompt = _trunk(x_prompt, p, **_tiles(x_prompt.shape[1]))
    y_sample = _trunk(x_sample, p, **_tiles(x_sample.shape[1]))
    return (y_prompt, y_sample)
```
